```python
import jax, jax.numpy as jnp
from jax import lax
import numpy as np

D_MODEL = 1024
BATCH = 2
SEQ = 16384
DEPTH = 2

CHUNK = 64
PLE_DIM = 256
BRANCH_WIDTH = D_MODEL // 2
N_BRANCH = 4
SG_BLOCK = 128
SG_GROUPS = 4
SG_WIDTH = BRANCH_WIDTH
GLA_HEADS = 4
GLA_DK = 64
GLA_DV = BRANCH_WIDTH // GLA_HEADS
GLA_RANK = 16
GLA_TAU = 16.0
ATT_HEADS = 8
ATT_HD = BRANCH_WIDTH // ATT_HEADS
ATT_BAND = 9
MAX_REL = 256
REL_TABLE = CHUNK + MAX_REL
CONV_WIDTH = BRANCH_WIDTH
CONV_K = 31
D_FF = 4 * D_MODEL
EPS = 1e-6
NEG_INF = -1e30

IN_SPLITS = (SG_WIDTH, SG_WIDTH,
             GLA_HEADS * GLA_DK, GLA_HEADS * GLA_DK, GLA_HEADS * GLA_DV, GLA_HEADS * GLA_DV, GLA_RANK,
             ATT_HEADS * ATT_HD, ATT_HEADS * ATT_HD, ATT_HEADS * ATT_HD,
             CONV_WIDTH, CONV_WIDTH)
IN_COLS = 2 * SG_WIDTH + 2 * GLA_HEADS * GLA_DK + 2 * GLA_HEADS * GLA_DV + GLA_RANK + 3 * ATT_HEADS * ATT_HD + 2 * CONV_WIDTH

kernel_name = "hybrid_gated_branch_streaming_encoder"


def rms_norm(x, g):
    xf = x.astype(jnp.float32)
    y = xf * lax.rsqrt(jnp.mean(xf * xf, axis=-1, keepdims=True) + EPS)
    return (y * g.astype(jnp.float32)).astype(x.dtype)


def layer_norm(x, g, b):
    xf = x.astype(jnp.float32)
    mu = jnp.mean(xf, axis=-1, keepdims=True)
    xc = xf - mu
    y = xc * lax.rsqrt(jnp.mean(xc * xc, axis=-1, keepdims=True) + EPS)
    return (y * g.astype(jnp.float32) + b.astype(jnp.float32)).astype(x.dtype)


def spatial_gating(u, v, ln_g, ln_b, w_s, b_s):
    bsz, s, _ = u.shape
    nb = s // SG_BLOCK
    cg = SG_WIDTH // SG_GROUPS
    v = layer_norm(v, ln_g, ln_b)
    vb = v.reshape(bsz, nb, SG_BLOCK, SG_GROUPS, cg)
    pos = jnp.arange(SG_BLOCK)
    mask = (pos[None, :] // CHUNK) <= (pos[:, None] // CHUNK)
    w = jnp.where(mask[None], w_s, 0.0)
    mixed = jnp.einsum('gij,bnjgc->bnigc', w, vb) + b_s.T[None, None, :, :, None]
    return u * mixed.reshape(bsz, s, SG_WIDTH)


def gated_linear_attention(q, k, v, r, a_lr, w_a2, b_a, norm_g):
    f32 = jnp.float32
    bsz, s, _ = q.shape
    nc = s // CHUNK
    qc = q.astype(f32).reshape(bsz, nc, CHUNK, GLA_HEADS, GLA_DK) * (GLA_DK ** -0.5)
    kc = k.astype(f32).reshape(bsz, nc, CHUNK, GLA_HEADS, GLA_DK)
    vc = v.astype(f32).reshape(bsz, nc, CHUNK, GLA_HEADS, GLA_DV)
    log_a = jax.nn.log_sigmoid(jnp.einsum('bsr,rk->bsk', a_lr.astype(f32), w_a2.astype(f32))
                               + b_a.astype(f32)) / GLA_TAU
    log_a = log_a.reshape(bsz, nc, CHUNK, GLA_HEADS, GLA_DK)
    cum = jnp.cumsum(log_a, axis=2)
    total = cum[:, :, -1]
    k_dec = kc * jnp.exp(total[:, :, None] - cum)
    upd = jnp.einsum('bclhd,bclhe->cbhde', k_dec, vc)
    decay = jnp.exp(total).transpose(1, 0, 2, 3)

    def step(state, inp):
        d, u_c = inp
        state = d[..., None] * state + u_c
        return state, state

    s0 = jnp.zeros((bsz, GLA_HEADS, GLA_DK, GLA_DV), f32)
    _, states = lax.scan(step, s0, (decay, upd))
    o = jnp.einsum('bclhd,cbhde->bclhe', qc, states)
    o = o * lax.rsqrt(jnp.mean(o * o, axis=-1, keepdims=True) + EPS)
    o = o.reshape(bsz, s, GLA_HEADS * GLA_DV) * norm_g.astype(f32)
    return (o * jax.nn.silu(r.astype(f32))).astype(r.dtype)


def band_chunk_attention(q, k, v, rel_bias):
    f32 = jnp.float32
    bsz, s, _ = q.shape
    nc = s // CHUNK
    prev = ATT_BAND - 1
    band = ATT_BAND * CHUNK
    qh = q.reshape(bsz, s, ATT_HEADS, ATT_HD)
    pad = ((0, 0), (prev * CHUNK, 0), (0, 0), (0, 0))
    kp = jnp.pad(k.reshape(bsz, s, ATT_HEADS, ATT_HD), pad)
    vp = jnp.pad(v.reshape(bsz, s, ATT_HEADS, ATT_HD), pad)
    l_idx = jnp.arange(CHUNK)
    m_idx = jnp.arange(band)
    rel = l_idx[:, None] + prev * CHUNK - m_idx[None, :]
    idx = jnp.clip(rel, -(CHUNK - 1), MAX_REL) + (CHUNK - 1)
    bias = rel_bias.astype(f32)[:, idx]
    scale = ATT_HD ** -0.5

    def one_chunk(c):
        qc = lax.dynamic_slice_in_dim(qh, c * CHUNK, CHUNK, axis=1)
        kc = lax.dynamic_slice_in_dim(kp, c * CHUNK, band, axis=1)
        vc = lax.dynamic_slice_in_dim(vp, c * CHUNK, band, axis=1)
        sc = jnp.einsum('blhd,bmhd->bhlm', qc, kc, preferred_element_type=f32) * scale + bias[None]
        key_ok = m_idx >= (prev - c) * CHUNK
        sc = jnp.where(key_ok[None, None, None, :], sc, NEG_INF)
        pw = jax.nn.softmax(sc, axis=-1)
        return jnp.einsum('bhlm,bmhd->blhd', pw.astype(vc.dtype), vc)

    out = lax.map(one_chunk, jnp.arange(nc))
    return out.transpose(1, 0, 2, 3, 4).reshape(bsz, s, ATT_HEADS * ATT_HD)


def conformer_conv(a, g, dw_w, dw_b, ln_g, ln_b):
    y = a * jax.nn.sigmoid(g)
    y = lax.conv_general_dilated(y, dw_w[:, None, :], window_strides=(1,),
                                 padding=((CONV_K - 1, 0),),
                                 dimension_numbers=('NWC', 'WIO', 'NWC'),
                                 feature_group_count=CONV_WIDTH) + dw_b
    return jax.nn.silu(layer_norm(y, ln_g, ln_b))


def setup_inputs(seed: int = 0) -> dict:
    key = jax.random.key(seed)
    ks = iter(jax.random.split(key, 40))

    def nrm(shape, scale):
        return scale * jax.random.normal(next(ks), shape, jnp.float32)

    def gain(shape):
        return 1.0 + nrm(shape, 0.05)

    L = DEPTH
    return {
        "x": nrm((BATCH, SEQ, D_MODEL), 1.0),
        "p": nrm((DEPTH, BATCH, SEQ, PLE_DIM), 1.0),
        "norm1_g": gain((L, D_MODEL)),
        "w_in": nrm((L, D_MODEL, IN_COLS), D_MODEL ** -0.5),
        "sg_ln_g": gain((L, SG_WIDTH)),
        "sg_ln_b": nrm((L, SG_WIDTH), 0.02),
        "sg_w": nrm((L, SG_GROUPS, SG_BLOCK, SG_BLOCK), SG_BLOCK ** -0.5),
        "sg_b": 1.0 + nrm((L, SG_GROUPS, SG_BLOCK), 0.1),
        "gla_w_a2": nrm((L, GLA_RANK, GLA_HEADS * GLA_DK), GLA_RANK ** -0.5),
        "gla_b_a": nrm((L, GLA_HEADS * GLA_DK), 0.1),
        "gla_norm_g": gain((L, GLA_HEADS * GLA_DV)),
        "att_rel_bias": nrm((L, ATT_HEADS, REL_TABLE), 0.5),
        "conv_dw_w": nrm((L, CONV_K, CONV_WIDTH), CONV_K ** -0.5),
        "conv_dw_b": nrm((L, CONV_WIDTH), 0.02),
        "conv_ln_g": gain((L, CONV_WIDTH)),
        "conv_ln_b": nrm((L, CONV_WIDTH), 0.02),
        "w_branch": nrm((L, N_BRANCH, BRANCH_WIDTH, D_MODEL), BRANCH_WIDTH ** -0.5),
        "w_gate": nrm((L, N_BRANCH, D_MODEL, D_MODEL), D_MODEL ** -0.5),
        "b_gate": nrm((L, N_BRANCH, D_MODEL), 0.02),
        "w_out": nrm((L, D_MODEL, D_MODEL), D_MODEL ** -0.5),
        "norm2_g": gain((L, D_MODEL)),
        "w_ff1": nrm((L, D_MODEL, D_FF), D_MODEL ** -0.5),
        "w_ff2": nrm((L, D_FF, D_MODEL), D_FF ** -0.5),
        "norm3_g": gain((L, D_MODEL)),
        "w_ple_gate": nrm((L, D_MODEL, D_MODEL), D_MODEL ** -0.5),
        "b_ple_gate": nrm((L, D_MODEL), 0.02),
        "w_ple": nrm((L, PLE_DIM, D_MODEL), PLE_DIM ** -0.5),
        "final_g": gain((D_MODEL,)),
    }


def reference(x, p, norm1_g, w_in, sg_ln_g, sg_ln_b, sg_w, sg_b, gla_w_a2, gla_b_a, gla_norm_g,
              att_rel_bias, conv_dw_w, conv_dw_b, conv_ln_g, conv_ln_b, w_branch, w_gate, b_gate,
              w_out, norm2_g, w_ff1, w_ff2, norm3_g, w_ple_gate, b_ple_gate, w_ple, final_g):
    cuts = np.cumsum(IN_SPLITS)[:-1].tolist()
    h = x
    for i in range(DEPTH):
        xn = rms_norm(h, norm1_g[i])
        proj = jnp.einsum('bsd,dk->bsk', xn, w_in[i])
        (sg_u, sg_v, g_q, g_k, g_v, g_r, g_a, a_q, a_k, a_v, c_a, c_g) = jnp.split(proj, cuts, axis=-1)

        y_a = spatial_gating(jax.nn.gelu(sg_u), jax.nn.gelu(sg_v), sg_ln_g[i], sg_ln_b[i], sg_w[i], sg_b[i])
        y_b = gated_linear_attention(g_q, g_k, g_v, g_r, g_a, gla_w_a2[i], gla_b_a[i], gla_norm_g[i])
        y_c = band_chunk_attention(a_q, a_k, a_v, att_rel_bias[i])
        y_d = conformer_conv(c_a, c_g, conv_dw_w[i], conv_dw_b[i], conv_ln_g[i], conv_ln_b[i])

        merged = jnp.zeros_like(h)
        for n, y in enumerate((y_a, y_b, y_c, y_d)):
            gate = jax.nn.sigmoid(jnp.einsum('bsd,de->bse', xn, w_gate[i, n]) + b_gate[i, n])
            merged = merged + gate * jnp.einsum('bsk,kd->bsd', y, w_branch[i, n])
        h = h + jnp.einsum('bsd,de->bse', merged, w_out[i])

        hn = rms_norm(h, norm2_g[i])
        ff = jnp.square(jax.nn.relu(jnp.einsum('bsd,df->bsf', hn, w_ff1[i])))
        h = h + jnp.einsum('bsf,fd->bsd', ff, w_ff2[i])

        hg = rms_norm(h, norm3_g[i])
        ple_gate = jax.nn.sigmoid(jnp.einsum('bsd,de->bse', hg, w_ple_gate[i]) + b_ple_gate[i])
        h = h + ple_gate * jnp.einsum('bsq,qd->bsd', p[i], w_ple[i])
    return rms_norm(h, final_g)
```

```python
import functools

import jax
import jax.numpy as jnp
from jax import lax
from jax.experimental import pallas as pl
from jax.experimental.pallas import tpu as pltpu

F32 = jnp.float32
BF16 = jnp.bfloat16

EPS = 1e-6
NEG_INF = -1e30

CHUNK = 64
SG_BLOCK = 128
SG_GROUPS = 4
GLA_HEADS = 4
GLA_DK = 64
GLA_DV = 128
GLA_RANK = 16
GLA_TAU = 16.0
ATT_HEADS = 8
ATT_HD = 64
ATT_BAND = 9
MAX_REL = 256
CONV_K = 31

V7X_LANES = 128
V7X_VMEM_BYTES = 64 * 1024 * 1024

TOK_TILE = 512
ATT_TILE = 256
ATT_WIN = 3 * ATT_TILE
CONV_TILE = 256
CONV_HALO = 32
CONV_RB = 32


def _cparams(vmem_bytes):
    return pltpu.CompilerParams(
        dimension_semantics=None,
        vmem_limit_bytes=int(min(vmem_bytes, V7X_VMEM_BYTES - (6 << 20))),
    )


def _const_spec(shape):
    nd = len(shape)
    return pl.BlockSpec(shape, lambda *_: (0,) * nd, pipeline_mode=pl.Buffered(1))


def _rms(x, g):
    ms = jnp.mean(x * x, axis=-1, keepdims=True)
    return x * lax.rsqrt(ms + EPS) * g


def _layer_norm(x, g, b):
    mu = jnp.mean(x, axis=-1, keepdims=True)
    xc = x - mu
    var = jnp.mean(xc * xc, axis=-1, keepdims=True)
    return xc * lax.rsqrt(var + EPS) * g + b


def _dot(a, b):
    return jnp.dot(a, b, preferred_element_type=F32)


def _dot_nt(a, b):
    return lax.dot_general(a, b, (((1,), (1,)), ((), ())), preferred_element_type=F32)


def _dot_tn(a, b):
    return lax.dot_general(a, b, (((0,), (0,)), ((), ())), preferred_element_type=F32)


def _sigmoid(x):
    return 1.0 / (1.0 + jnp.exp(-x))


def _silu(x):
    return x * _sigmoid(x)


def _log_sigmoid(x):
    return jnp.minimum(x, 0.0) - jnp.log(1.0 + jnp.exp(-jnp.abs(x)))


def _inproj_kernel(h_ref, g1_ref, wsg_ref, wgla_ref, wga_ref, watt_ref, wcv_ref,
                   lng_ref, lnb_ref, sgw_ref, sgbias_ref, wa2_ref, ba_ref,
                   ya_ref, gq_ref, gk_ref, gv_ref, gr_ref, la_ref,
                   aq_ref, ak_ref, av_ref, cy_ref):
    tile = h_ref.shape[0]
    xn = _rms(h_ref[...], g1_ref[...]).astype(BF16)

    ps = _dot(xn, wsg_ref[...])
    half = ps.shape[1] // 2
    u = jax.nn.gelu(ps[:, :half])
    vn = _layer_norm(jax.nn.gelu(ps[:, half:]), lng_ref[...], lnb_ref[...]).astype(BF16)
    pi = lax.broadcasted_iota(jnp.int32, (SG_BLOCK, SG_BLOCK), 0) // CHUNK
    pj = lax.broadcasted_iota(jnp.int32, (SG_BLOCK, SG_BLOCK), 1) // CHUNK
    cg = half // SG_GROUPS
    for g in range(SG_GROUPS):
        wm = jnp.where(pj <= pi, sgw_ref[g], 0.0).astype(BF16)
        cols = slice(g * cg, (g + 1) * cg)
        for n in range(tile // SG_BLOCK):
            rows = slice(n * SG_BLOCK, (n + 1) * SG_BLOCK)
            mixed = _dot(wm, vn[rows, cols]) + sgbias_ref[:, cols]
            ya_ref[rows, cols] = (u[rows, cols] * mixed).astype(ya_ref.dtype)

    pg = _dot(xn, wgla_ref[...])
    nqk = GLA_HEADS * GLA_DK
    nv = GLA_HEADS * GLA_DV
    gq_ref[...] = (pg[:, :nqk] * (GLA_DK ** -0.5)).astype(gq_ref.dtype)
    gk_ref[...] = pg[:, nqk:2 * nqk]
    gv_ref[...] = pg[:, 2 * nqk:2 * nqk + nv].astype(gv_ref.dtype)
    gr_ref[...] = pg[:, 2 * nqk + nv:]
    a_lr = _dot(xn, wga_ref[...]).astype(BF16)
    la_ref[...] = _log_sigmoid(_dot(a_lr, wa2_ref[...]) + ba_ref[...]) / GLA_TAU

    pa = _dot(xn, watt_ref[...])
    na = ATT_HEADS * ATT_HD
    aq_ref[...] = (pa[:, :na] * (ATT_HD ** -0.5)).astype(aq_ref.dtype)
    ak_ref[...] = pa[:, na:2 * na].astype(ak_ref.dtype)
    av_ref[...] = pa[:, 2 * na:].astype(av_ref.dtype)

    pc = _dot(xn, wcv_ref[...])
    cw = pc.shape[1] // 2
    cy_ref[...] = pc[:, :cw] * _sigmoid(pc[:, cw:])


def _in_proj(h, g1, wsg, wgla, wga, watt, wcv, lng, lnb, sgw, sgbias, wa2, ba):
    n, d = h.shape
    t = TOK_TILE
    bw = wsg.shape[1] // 2
    nqk = GLA_HEADS * GLA_DK
    row = lambda i: (i, 0)
    outs = [
        ((n, bw), BF16),
        ((n, nqk), BF16),
        ((n, nqk), F32),
        ((n, bw), BF16),
        ((n, bw), F32),
        ((n, nqk), F32),
        ((n, bw), BF16),
        ((n, bw), BF16),
        ((n, bw), BF16),
        ((n, bw), F32),
    ]
    consts = (g1, wsg, wgla, wga, watt, wcv, lng, lnb, sgw, sgbias, wa2, ba)
    return pl.pallas_call(
        _inproj_kernel,
        grid=(n // t,),
        in_specs=[pl.BlockSpec((t, d), row)] + [_const_spec(c.shape) for c in consts],
        out_specs=[pl.BlockSpec((t, s[1]), row) for s, _ in outs],
        out_shape=[jax.ShapeDtypeStruct(s, dt) for s, dt in outs],
        compiler_params=_cparams(52 << 20),
        name="in_proj",
    )(h, *consts)


def _gla_kernel(q_ref, k_ref, v_ref, r_ref, la_ref, ng_ref, y_ref, st_ref, kd_ref, sfx_ref, o_ref):
    tile = q_ref.shape[0]
    nqk = GLA_HEADS * GLA_DK
    nv = GLA_HEADS * GLA_DV

    @pl.when(pl.program_id(1) == 0)
    def _():
        st_ref[...] = jnp.zeros_like(st_ref)

    la = la_ref[...]
    hi = la.astype(BF16)
    lo = (la - hi.astype(F32)).astype(BF16)
    ri = lax.broadcasted_iota(jnp.int32, (tile, tile), 0)
    ci = lax.broadcasted_iota(jnp.int32, (tile, tile), 1)
    later = jnp.where((ci > ri) & (ci // CHUNK == ri // CHUNK), 1.0, 0.0).astype(BF16)
    sfx = _dot(later, hi) + _dot(later, lo)
    sfx_ref[...] = sfx
    kd_ref[...] = (k_ref[...] * jnp.exp(sfx)).astype(BF16)

    si = lax.broadcasted_iota(jnp.int32, (nv, nqk), 0) // GLA_DV
    sj = lax.broadcasted_iota(jnp.int32, (nv, nqk), 1) // GLA_DK
    same_head = si == sj

    def chunk(c, carry):
        r0 = pl.multiple_of(c * CHUNK, CHUNK)
        total = la_ref[pl.ds(r0, 1), :] + sfx_ref[pl.ds(r0, 1), :]
        upd = _dot_tn(v_ref[pl.ds(r0, CHUNK), :], kd_ref[pl.ds(r0, CHUNK), :])
        st = jnp.where(same_head, st_ref[...] * jnp.exp(total) + upd, 0.0)
        st_ref[...] = st
        o_ref[pl.ds(r0, CHUNK), :] = _dot_nt(q_ref[pl.ds(r0, CHUNK), :], st.astype(BF16))
        return carry

    lax.fori_loop(0, tile // CHUNK, chunk, 0)

    for h in range(GLA_HEADS):
        cols = slice(h * GLA_DV, (h + 1) * GLA_DV)
        o = o_ref[:, cols]
        o = o * lax.rsqrt(jnp.mean(o * o, axis=-1, keepdims=True) + EPS) * ng_ref[:, cols]
        y_ref[:, cols] = (o * _silu(r_ref[:, cols])).astype(y_ref.dtype)


def _gla(q, k, v, r, la, ng, bsz):
    n = q.shape[0]
    t = TOK_TILE
    nt = n // bsz // t
    nqk = GLA_HEADS * GLA_DK
    nv = GLA_HEADS * GLA_DV
    row = lambda b, i: (b * nt + i, 0)
    return pl.pallas_call(
        _gla_kernel,
        grid=(bsz, nt),
        in_specs=[pl.BlockSpec((t, nqk), row), pl.BlockSpec((t, nqk), row),
                  pl.BlockSpec((t, nv), row), pl.BlockSpec((t, nv), row),
                  pl.BlockSpec((t, nqk), row), _const_spec(ng.shape)],
        out_specs=pl.BlockSpec((t, nv), row),
        out_shape=jax.ShapeDtypeStruct((n, nv), BF16),
        scratch_shapes=[pltpu.VMEM((nv, nqk), F32), pltpu.VMEM((t, nqk), BF16),
                        pltpu.VMEM((t, nqk), F32), pltpu.VMEM((t, nv), F32)],
        compiler_params=_cparams(32 << 20),
        name="gla",
    )(q, k, v, r, la, ng)


def _attn_kernel(q_ref, k0_ref, k1_ref, k2_ref, v0_ref, v1_ref, v2_ref, bias_ref, y_ref):
    t = pl.program_id(1)
    tq = q_ref.shape[0]
    kband = jnp.concatenate([k0_ref[...], k1_ref[...], k2_ref[...]], axis=0)
    vband = jnp.concatenate([v0_ref[...], v1_ref[...], v2_ref[...]], axis=0)
    win = kband.shape[0]
    key_ok = lax.broadcasted_iota(jnp.int32, (1, win), 1) >= (2 - t) * tq
    upper = lax.broadcasted_iota(jnp.int32, (1, V7X_LANES), 1) >= ATT_HD
    for pair in range(ATT_HEADS // 2):
        cols = slice(pair * V7X_LANES, (pair + 1) * V7X_LANES)
        qp = q_ref[:, cols]
        kp = kband[:, cols]
        vp = vband[:, cols]
        acc = jnp.zeros((tq, V7X_LANES), F32)
        for hh in range(2):
            mine = upper if hh else jnp.logical_not(upper)
            s = _dot_nt(jnp.where(mine, qp, jnp.zeros_like(qp)), kp) + bias_ref[2 * pair + hh]
            s = jnp.where(key_ok, s, NEG_INF)
            e = jnp.exp(s - jnp.max(s, axis=-1, keepdims=True))
            den = jnp.sum(e, axis=-1, keepdims=True)
            pv = _dot(e.astype(BF16), jnp.where(mine, vp, jnp.zeros_like(vp)))
            acc = acc + pv / den
        y_ref[:, cols] = acc.astype(y_ref.dtype)


def _attn(q, k, v, bias, bsz):
    n, w = q.shape
    t = ATT_TILE
    nt = n // bsz // t
    cur = lambda b, i: (b * nt + i, 0)
    prev1 = lambda b, i: (b * nt + jnp.maximum(i - 1, 0), 0)
    prev2 = lambda b, i: (b * nt + jnp.maximum(i - 2, 0), 0)
    blk = lambda im: pl.BlockSpec((t, w), im)
    return pl.pallas_call(
        _attn_kernel,
        grid=(bsz, nt),
        in_specs=[blk(cur), blk(prev2), blk(prev1), blk(cur), blk(prev2), blk(prev1), blk(cur),
                  _const_spec(bias.shape)],
        out_specs=blk(cur),
        out_shape=jax.ShapeDtypeStruct((n, w), BF16),
        compiler_params=_cparams(40 << 20),
        name="attn",
    )(q, k, k, k, v, v, v, bias)


def _att_bias_table(rel_bias):
    ql = jnp.arange(ATT_TILE)[:, None]
    kl = jnp.arange(ATT_WIN)[None, :]
    rel = ql - kl + 2 * ATT_TILE
    idx = jnp.clip(rel, -(CHUNK - 1), MAX_REL) + (CHUNK - 1)
    qc = ql // CHUNK + 2 * ATT_TILE // CHUNK
    kc = kl // CHUNK
    in_band = (kc <= qc) & (kc >= qc - (ATT_BAND - 1))
    return jnp.where(in_band[None], rel_bias.astype(F32)[:, idx], NEG_INF)


def _conv_kernel(x_ref, w_ref, b_ref, lng_ref, lnb_ref, y_ref, ext_ref, sh_ref):
    tile = x_ref.shape[0]

    @pl.when(pl.program_id(1) == 0)
    def _():
        ext_ref[0:CONV_HALO, :] = jnp.zeros((CONV_HALO, ext_ref.shape[1]), F32)

    ext_ref[CONV_HALO:, :] = x_ref[...]
    base = CONV_HALO - (CONV_K - 1)
    for ph in range(8):
        rows = tile + 8 * ((CONV_K - 1 - ph) // 8)
        sh_ref[ph, 0:rows, :] = ext_ref[base + ph:base + ph + rows, :]

    def block(rb, carry):
        r0 = pl.multiple_of(rb * CONV_RB, CONV_RB)
        acc = jnp.zeros((CONV_RB, x_ref.shape[1]), F32)
        for j in range(CONV_K):
            acc = acc + w_ref[j:j + 1, :] * sh_ref[j % 8, pl.ds(r0 + 8 * (j // 8), CONV_RB), :]
        y = _layer_norm(acc + b_ref[...], lng_ref[...], lnb_ref[...])
        y_ref[pl.ds(r0, CONV_RB), :] = _silu(y).astype(y_ref.dtype)
        return carry

    lax.fori_loop(0, tile // CONV_RB, block, 0)
    ext_ref[0:CONV_HALO, :] = ext_ref[tile:tile + CONV_HALO, :]


def _conv(x, w, b, lng, lnb, bsz):
    n, c = x.shape
    t = CONV_TILE
    nt = n // bsz // t
    row = lambda bb, i: (bb * nt + i, 0)
    span = t + 8 * ((CONV_K - 1) // 8)
    return pl.pallas_call(
        _conv_kernel,
        grid=(bsz, nt),
        in_specs=[pl.BlockSpec((t, c), row)] + [_const_spec(a.shape) for a in (w, b, lng, lnb)],
        out_specs=pl.BlockSpec((t, c), row),
        out_shape=jax.ShapeDtypeStruct((n, c), BF16),
        scratch_shapes=[pltpu.VMEM((t + CONV_HALO, c), F32), pltpu.VMEM((8, span, c), F32)],
        compiler_params=_cparams(24 << 20),
        name="conv",
    )(x, w, b, lng, lnb)


def _merge_kernel(h_ref, ya_ref, yb_ref, yc_ref, yd_ref, g1_ref, wg_ref, bg_ref, wb_ref, wo_ref, o_ref):
    h = h_ref[...]
    xn = _rms(h, g1_ref[...]).astype(BF16)
    merged = None
    for n, y_ref in enumerate((ya_ref, yb_ref, yc_ref, yd_ref)):
        gate = _sigmoid(_dot(xn, wg_ref[n]) + bg_ref[n])
        term = gate * _dot(y_ref[...], wb_ref[n])
        merged = term if merged is None else merged + term
    o_ref[...] = h + _dot(merged.astype(BF16), wo_ref[...])


def _merge(h, ys, g1, wg, bg, wb, wo):
    n, d = h.shape
    t = TOK_TILE
    row = lambda i: (i, 0)
    consts = (g1, wg, bg, wb, wo)
    return pl.pallas_call(
        _merge_kernel,
        grid=(n // t,),
        in_specs=[pl.BlockSpec((t, d), row)] + [pl.BlockSpec((t, y.shape[1]), row) for y in ys]
                 + [_const_spec(c.shape) for c in consts],
        out_specs=pl.BlockSpec((t, d), row),
        out_shape=jax.ShapeDtypeStruct((n, d), F32),
        compiler_params=_cparams(48 << 20),
        name="merge",
    )(h, *ys, *consts)


def _ffn_ple_kernel(h_ref, p_ref, g2_ref, w1_ref, w2_ref, g3_ref, wpg_ref, bpg_ref, wp_ref, gf_ref,
                    o_ref, *, final_norm, ff_chunk):
    h = h_ref[...]
    hn = _rms(h, g2_ref[...]).astype(BF16)
    acc = h
    for j in range(w1_ref.shape[1] // ff_chunk):
        cols = slice(j * ff_chunk, (j + 1) * ff_chunk)
        a = jnp.maximum(_dot(hn, w1_ref[:, cols]), 0.0)
        acc = acc + _dot((a * a).astype(BF16), w2_ref[cols, :])
    hg = _rms(acc, g3_ref[...]).astype(BF16)
    gate = _sigmoid(_dot(hg, wpg_ref[...]) + bpg_ref[...])
    out = acc + gate * _dot(p_ref[...].astype(BF16), wp_ref[...])
    if final_norm:
        out = _rms(out, gf_ref[...])
    o_ref[...] = out


def _ffn_ple(h, p, g2, w1, w2, g3, wpg, bpg, wp, gf, final_norm):
    n, d = h.shape
    t = TOK_TILE
    row = lambda i: (i, 0)
    consts = (g2, w1, w2, g3, wpg, bpg, wp, gf)
    return pl.pallas_call(
        functools.partial(_ffn_ple_kernel, final_norm=final_norm, ff_chunk=1024),
        grid=(n // t,),
        in_specs=[pl.BlockSpec((t, d), row), pl.BlockSpec((t, p.shape[1]), row)]
                 + [_const_spec(c.shape) for c in consts],
        out_specs=pl.BlockSpec((t, d), row),
        out_shape=jax.ShapeDtypeStruct((n, d), F32),
        compiler_params=_cparams(52 << 20),
        name="ffn_ple",
    )(h, p, *consts)


def kernel(x, p, norm1_g, w_in, sg_ln_g, sg_ln_b, sg_w, sg_b, gla_w_a2, gla_b_a, gla_norm_g, att_rel_bias, conv_dw_w, conv_dw_b, conv_ln_g, conv_ln_b, w_branch, w_gate, b_gate, w_out, norm2_g, w_ff1, w_ff2, norm3_g, w_ple_gate, b_ple_gate, w_ple, final_g):
    bsz, seq, d = x.shape
    depth = w_in.shape[0]
    n = bsz * seq
    bw = d // 2
    nqk = GLA_HEADS * GLA_DK
    assert seq % TOK_TILE == 0 and seq % ATT_TILE == 0 and bw == SG_GROUPS * V7X_LANES
    row2 = lambda a: a.reshape(1, -1)

    c_sg = 2 * bw
    c_gla = c_sg + 2 * nqk + 2 * bw
    c_ga = c_gla + GLA_RANK
    c_att = c_ga + 3 * bw

    h = x.reshape(n, d)
    for i in range(depth):
        w = w_in[i].astype(BF16)
        wga = jnp.pad(w[:, c_gla:c_ga], ((0, 0), (0, V7X_LANES - GLA_RANK)))
        wa2 = jnp.pad(gla_w_a2[i].astype(BF16), ((0, V7X_LANES - GLA_RANK), (0, 0)))
        sgbias = jnp.repeat(sg_b[i].T, bw // SG_GROUPS, axis=1)
        g1 = row2(norm1_g[i])
        (y_a, gq, gk, gv, gr, la, aq, ak, av, cy) = _in_proj(
            h, g1, w[:, :c_sg], w[:, c_sg:c_gla], wga, w[:, c_ga:c_att], w[:, c_att:],
            row2(sg_ln_g[i]), row2(sg_ln_b[i]), sg_w[i], sgbias, wa2, row2(gla_b_a[i]))
        y_b = _gla(gq, gk, gv, gr, la, row2(gla_norm_g[i]), bsz)
        y_c = _attn(aq, ak, av, _att_bias_table(att_rel_bias[i]), bsz)
        y_d = _conv(cy, jnp.pad(conv_dw_w[i], ((0, 1), (0, 0))), row2(conv_dw_b[i]),
                    row2(conv_ln_g[i]), row2(conv_ln_b[i]), bsz)
        h = _merge(h, (y_a, y_b, y_c, y_d), g1, w_gate[i].astype(BF16), b_gate[i][:, None, :],
                   w_branch[i].astype(BF16), w_out[i].astype(BF16))
        h = _ffn_ple(h, p[i].reshape(n, -1), row2(norm2_g[i]), w_ff1[i].astype(BF16),
                     w_ff2[i].astype(BF16), row2(norm3_g[i]), w_ple_gate[i].astype(BF16),
                     row2(b_ple_gate[i]), w_ple[i].astype(BF16), row2(final_g),
                     final_norm=(i == depth - 1))
    return h.reshape(bsz, seq, d)
```

```python
import functools

import jax
import jax.numpy as jnp
from jax import lax
from jax.experimental import pallas as pl
from jax.experimental.pallas import tpu as pltpu

F32 = jnp.float32
BF16 = jnp.bfloat16

EPS = 1e-6
NEG_INF = -1e30
LOG2E = 1.4426950408889634

CHUNK = 64
SG_BLOCK = 128
SG_GROUPS = 4
GLA_HEADS = 4
GLA_DK = 64
GLA_DV = 128
GLA_RANK = 16
GLA_TAU = 16.0
ATT_HEADS = 8
ATT_HD = 64
ATT_BAND = 9
MAX_REL = 256
CONV_K = 31

V7X_LANES = 128
V7X_VMEM_BYTES = 64 * 1024 * 1024

TOK_TILE = 512
ATT_TILE = 256
ATT_WIN = 3 * ATT_TILE
ATT_GROUP = 4
CONV_TILE = 512
CONV_HALO = 64
CONV_SEG = 68
CONV_IB = 4
CONV_NB = 64


def _cparams(vmem_bytes):
    return pltpu.CompilerParams(
        dimension_semantics=None,
        vmem_limit_bytes=int(min(vmem_bytes, V7X_VMEM_BYTES - (6 << 20))),
    )


def _const_spec(shape):
    nd = len(shape)
    return pl.BlockSpec(shape, lambda *_: (0,) * nd, pipeline_mode=pl.Buffered(1))


def _rms(x, g):
    ms = jnp.mean(x * x, axis=-1, keepdims=True)
    return x * lax.rsqrt(ms + EPS) * g


def _layer_norm(x, g, b):
    mu = jnp.mean(x, axis=-1, keepdims=True)
    xc = x - mu
    var = jnp.mean(xc * xc, axis=-1, keepdims=True)
    return xc * lax.rsqrt(var + EPS) * g + b


def _dot(a, b):
    return jnp.dot(a, b, preferred_element_type=F32)


def _dot_nt(a, b):
    return lax.dot_general(a, b, (((1,), (1,)), ((), ())), preferred_element_type=F32)


def _dot_tn(a, b):
    return lax.dot_general(a, b, (((0,), (0,)), ((), ())), preferred_element_type=F32)


def _sigmoid(x):
    return 0.5 * jnp.tanh(0.5 * x) + 0.5


def _silu(x):
    return x * _sigmoid(x)


def _log_sigmoid(x):
    return jnp.minimum(x, 0.0) - jnp.log(1.0 + jnp.exp(-jnp.abs(x)))


def _inproj_kernel(h_ref, g1_ref, wsg_ref, wgla_ref, wga_ref, watt_ref, wcv_ref,
                   lng_ref, lnb_ref, sgw_ref, sgbias_ref, wa2_ref, ba_ref,
                   ya_ref, gq_ref, gk_ref, gv_ref, gr_ref, la_ref,
                   aq_ref, ak_ref, av_ref, cy_ref):
    tile = h_ref.shape[0]
    xn = _rms(h_ref[...], g1_ref[...]).astype(BF16)

    ps = _dot(xn, wsg_ref[...])
    half = ps.shape[1] // 2
    u = jax.nn.gelu(ps[:, :half])
    vn = _layer_norm(jax.nn.gelu(ps[:, half:]), lng_ref[...], lnb_ref[...]).astype(BF16)
    pi = lax.broadcasted_iota(jnp.int32, (SG_BLOCK, SG_BLOCK), 0) // CHUNK
    pj = lax.broadcasted_iota(jnp.int32, (SG_BLOCK, SG_BLOCK), 1) // CHUNK
    cg = half // SG_GROUPS
    for g in range(SG_GROUPS):
        wm = jnp.where(pj <= pi, sgw_ref[g], 0.0).astype(BF16)
        cols = slice(g * cg, (g + 1) * cg)
        for n in range(tile // SG_BLOCK):
            rows = slice(n * SG_BLOCK, (n + 1) * SG_BLOCK)
            mixed = _dot(wm, vn[rows, cols]) + sgbias_ref[:, cols]
            ya_ref[rows, cols] = (u[rows, cols] * mixed).astype(ya_ref.dtype)

    pg = _dot(xn, wgla_ref[...])
    nqk = GLA_HEADS * GLA_DK
    nv = GLA_HEADS * GLA_DV
    gq_ref[...] = (pg[:, :nqk] * (GLA_DK ** -0.5)).astype(gq_ref.dtype)
    gk_ref[...] = pg[:, nqk:2 * nqk]
    gv_ref[...] = pg[:, 2 * nqk:2 * nqk + nv].astype(gv_ref.dtype)
    gr_ref[...] = pg[:, 2 * nqk + nv:]
    a_lr = _dot(xn, wga_ref[...]).astype(BF16)
    la_ref[...] = _log_sigmoid(_dot(a_lr, wa2_ref[...]) + ba_ref[...]) / GLA_TAU

    pa = _dot(xn, watt_ref[...])
    na = ATT_HEADS * ATT_HD
    aq_ref[...] = (pa[:, :na] * (ATT_HD ** -0.5 * LOG2E)).astype(aq_ref.dtype)
    ak_ref[...] = pa[:, na:2 * na].astype(ak_ref.dtype)
    av_ref[...] = pa[:, 2 * na:].astype(av_ref.dtype)

    pc = _dot(xn, wcv_ref[...])
    cw = pc.shape[1] // 2
    cy_ref[...] = pc[:, :cw] * _sigmoid(pc[:, cw:])


def _in_proj(h, g1, wsg, wgla, wga, watt, wcv, lng, lnb, sgw, sgbias, wa2, ba):
    n, d = h.shape
    t = TOK_TILE
    bw = wsg.shape[1] // 2
    nqk = GLA_HEADS * GLA_DK
    row = lambda i: (i, 0)
    outs = [
        ((n, bw), BF16),
        ((n, nqk), BF16),
        ((n, nqk), F32),
        ((n, bw), BF16),
        ((n, bw), F32),
        ((n, nqk), F32),
        ((n, bw), BF16),
        ((n, bw), BF16),
        ((n, bw), BF16),
        ((n, bw), F32),
    ]
    consts = (g1, wsg, wgla, wga, watt, wcv, lng, lnb, sgw, sgbias, wa2, ba)
    return pl.pallas_call(
        _inproj_kernel,
        grid=(n // t,),
        in_specs=[pl.BlockSpec((t, d), row)] + [_const_spec(c.shape) for c in consts],
        out_specs=[pl.BlockSpec((t, s[1]), row) for s, _ in outs],
        out_shape=[jax.ShapeDtypeStruct(s, dt) for s, dt in outs],
        compiler_params=_cparams(52 << 20),
        name="in_proj",
    )(h, *consts)


def _gla_kernel(q_ref, k_ref, v_ref, r_ref, la_ref, ng_ref, y_ref, st_ref, kd_ref, sfx_ref, o_ref):
    tile = q_ref.shape[0]
    nqk = GLA_HEADS * GLA_DK
    nv = GLA_HEADS * GLA_DV

    @pl.when(pl.program_id(1) == 0)
    def _():
        st_ref[...] = jnp.zeros_like(st_ref)

    la = la_ref[...]
    hi = la.astype(BF16)
    lo = (la - hi.astype(F32)).astype(BF16)
    ri = lax.broadcasted_iota(jnp.int32, (tile, tile), 0)
    ci = lax.broadcasted_iota(jnp.int32, (tile, tile), 1)
    later = jnp.where((ci > ri) & (ci // CHUNK == ri // CHUNK), 1.0, 0.0).astype(BF16)
    sfx = _dot(later, hi) + _dot(later, lo)
    sfx_ref[...] = sfx
    kd_ref[...] = (k_ref[...] * jnp.exp(sfx)).astype(BF16)

    si = lax.broadcasted_iota(jnp.int32, (nv, nqk), 0) // GLA_DV
    sj = lax.broadcasted_iota(jnp.int32, (nv, nqk), 1) // GLA_DK
    same_head = si == sj

    def chunk(c, carry):
        r0 = pl.multiple_of(c * CHUNK, CHUNK)
        total = la_ref[pl.ds(r0, 1), :] + sfx_ref[pl.ds(r0, 1), :]
        upd = _dot_tn(v_ref[pl.ds(r0, CHUNK), :], kd_ref[pl.ds(r0, CHUNK), :])
        st = jnp.where(same_head, st_ref[...] * jnp.exp(total) + upd, 0.0)
        st_ref[...] = st
        o_ref[pl.ds(r0, CHUNK), :] = _dot_nt(q_ref[pl.ds(r0, CHUNK), :], st.astype(BF16))
        return carry

    lax.fori_loop(0, tile // CHUNK, chunk, 0)

    for h in range(GLA_HEADS):
        cols = slice(h * GLA_DV, (h + 1) * GLA_DV)
        o = o_ref[:, cols]
        o = o * lax.rsqrt(jnp.mean(o * o, axis=-1, keepdims=True) + EPS) * ng_ref[:, cols]
        y_ref[:, cols] = (o * _silu(r_ref[:, cols])).astype(y_ref.dtype)


def _gla(q, k, v, r, la, ng, bsz):
    n = q.shape[0]
    t = TOK_TILE
    nt = n // bsz // t
    nqk = GLA_HEADS * GLA_DK
    nv = GLA_HEADS * GLA_DV
    row = lambda b, i: (b * nt + i, 0)
    return pl.pallas_call(
        _gla_kernel,
        grid=(bsz, nt),
        in_specs=[pl.BlockSpec((t, nqk), row), pl.BlockSpec((t, nqk), row),
                  pl.BlockSpec((t, nv), row), pl.BlockSpec((t, nv), row),
                  pl.BlockSpec((t, nqk), row), _const_spec(ng.shape)],
        out_specs=pl.BlockSpec((t, nv), row),
        out_shape=jax.ShapeDtypeStruct((n, nv), BF16),
        scratch_shapes=[pltpu.VMEM((nv, nqk), F32), pltpu.VMEM((t, nqk), BF16),
                        pltpu.VMEM((t, nqk), F32), pltpu.VMEM((t, nv), F32)],
        compiler_params=_cparams(32 << 20),
        name="gla",
    )(q, k, v, r, la, ng)


def _attn_kernel(q_ref, k0_ref, k1_ref, k2_ref, v0_ref, v1_ref, v2_ref, bias_ref, y_ref):
    t = pl.program_id(1)
    tq = q_ref.shape[0]
    win = ATT_WIN

    def body(first_tiles):
        krefs = (k0_ref, k1_ref, k2_ref)
        vrefs = (v0_ref, v1_ref, v2_ref)
        key_ok = lax.broadcasted_iota(jnp.int32, (1, win), 1) >= (2 - t) * tq
        head_of_lane = lax.broadcasted_iota(jnp.int32, (1, ATT_GROUP * ATT_HD), 1) // ATT_HD
        for grp in range(ATT_HEADS // ATT_GROUP):
            cols = slice(grp * ATT_GROUP * ATT_HD, (grp + 1) * ATT_GROUP * ATT_HD)
            qg = q_ref[:, cols]
            acc = None
            for hh in range(ATT_GROUP):
                mine = head_of_lane == hh
                qm = jnp.where(mine, qg, jnp.zeros_like(qg))
                s = jnp.concatenate([_dot_nt(qm, kr[:, cols]) for kr in krefs], axis=1)
                s = s + bias_ref[grp * ATT_GROUP + hh]
                if first_tiles:
                    s = jnp.where(key_ok, s, NEG_INF)
                e = jnp.exp2(s - jnp.max(s, axis=-1, keepdims=True))
                r = 1.0 / jnp.sum(e, axis=-1, keepdims=True)
                e = e.astype(BF16)
                pv = sum(_dot(e[:, j * tq:(j + 1) * tq], vr[:, cols]) for j, vr in enumerate(vrefs)) * r
                acc = pv if acc is None else jnp.where(mine, pv, acc)
            y_ref[:, cols] = acc.astype(y_ref.dtype)

    pl.when(t < 2)(functools.partial(body, True))
    pl.when(t >= 2)(functools.partial(body, False))


def _attn(q, k, v, bias, bsz):
    n, w = q.shape
    t = ATT_TILE
    nt = n // bsz // t
    cur = lambda b, i: (b * nt + i, 0)
    prev1 = lambda b, i: (b * nt + jnp.maximum(i - 1, 0), 0)
    prev2 = lambda b, i: (b * nt + jnp.maximum(i - 2, 0), 0)
    blk = lambda im: pl.BlockSpec((t, w), im)
    return pl.pallas_call(
        _attn_kernel,
        grid=(bsz, nt),
        in_specs=[blk(cur), blk(prev2), blk(prev1), blk(cur), blk(prev2), blk(prev1), blk(cur),
                  _const_spec(bias.shape)],
        out_specs=blk(cur),
        out_shape=jax.ShapeDtypeStruct((n, w), BF16),
        compiler_params=_cparams(40 << 20),
        name="attn",
    )(q, k, k, k, v, v, v, bias)


def _att_bias_table(rel_bias):
    nh, ntab = rel_bias.shape
    period = ATT_TILE + ATT_WIN
    n_lo = (ATT_TILE - 1) - (CHUNK - 1)
    rb = rel_bias.astype(F32)
    vec = jnp.concatenate([jnp.broadcast_to(rb[:, :1], (nh, n_lo)), rb,
                           jnp.broadcast_to(rb[:, -1:], (nh, period - n_lo - ntab))], axis=1)
    rolled = jnp.roll(vec[:, ::-1], -ATT_TILE, axis=1)
    flat = jnp.tile(rolled, (1, ATT_TILE))[:, :ATT_TILE * (period - 1)]
    table = flat.reshape(nh, ATT_TILE, period - 1)[:, :, :ATT_WIN]
    qc = jnp.arange(ATT_TILE)[:, None] // CHUNK + 2 * ATT_TILE // CHUNK
    kc = jnp.arange(ATT_WIN)[None, :] // CHUNK
    in_band = (kc <= qc) & (kc >= qc - (ATT_BAND - 1))
    return jnp.where(in_band[None], table * LOG2E, NEG_INF)


def _conv_kernel(x_ref, w_ref, b_ref, lng_ref, lnb_ref, y_ref, ext_ref, co_ref):
    tile, width = x_ref.shape
    nslab = width // V7X_LANES
    out0 = CONV_HALO + tile - 8 * CONV_SEG
    in0 = out0 - (CONV_K - 1)

    @pl.when(pl.program_id(1) == 0)
    def _():
        ext_ref[:, 0:CONV_HALO, :] = jnp.zeros((nslab, CONV_HALO, V7X_LANES), F32)

    for c in range(nslab):
        ext_ref[c, CONV_HALO:, :] = x_ref[:, c * V7X_LANES:(c + 1) * V7X_LANES]

    for c in range(nslab):
        cols = slice(c * V7X_LANES, (c + 1) * V7X_LANES)
        wv = [jnp.broadcast_to(w_ref[j:j + 1, cols], (8, V7X_LANES)) for j in range(CONV_K)]

        def block(ib, carry, c=c, wv=wv):
            i0 = ib * CONV_IB
            xs = [ext_ref[c, pl.ds(in0 + i0 + k, 8, stride=CONV_SEG), :]
                  for k in range(CONV_IB + CONV_K - 1)]
            for a in range(CONV_IB):
                acc = wv[0] * xs[a]
                for j in range(1, CONV_K):
                    acc = acc + wv[j] * xs[a + j]
                co_ref[c, pl.ds(out0 + i0 + a, 8, stride=CONV_SEG), :] = acc
            return carry

        lax.fori_loop(0, CONV_SEG // CONV_IB, block, 0)

    lanes = [slice(c * V7X_LANES, (c + 1) * V7X_LANES) for c in range(nslab)]

    def norm_block(rb, carry):
        r0 = pl.multiple_of(rb * CONV_NB, CONV_NB)
        parts = [co_ref[c, pl.ds(CONV_HALO + r0, CONV_NB), :] + b_ref[:, lanes[c]] for c in range(nslab)]
        mu = jnp.sum(sum(parts), axis=-1, keepdims=True) / width
        cen = [q - mu for q in parts]
        var = jnp.sum(sum(q * q for q in cen), axis=-1, keepdims=True) / width
        inv = lax.rsqrt(var + EPS)
        for c in range(nslab):
            y = cen[c] * inv * lng_ref[:, lanes[c]] + lnb_ref[:, lanes[c]]
            y_ref[pl.ds(r0, CONV_NB), lanes[c]] = _silu(y).astype(y_ref.dtype)
        return carry

    lax.fori_loop(0, tile // CONV_NB, norm_block, 0, unroll=2)
    for c in range(nslab):
        ext_ref[c, 0:CONV_HALO, :] = ext_ref[c, tile:tile + CONV_HALO, :]


def _conv(x, w, b, lng, lnb, bsz):
    n, c = x.shape
    t = CONV_TILE
    nt = n // bsz // t
    row = lambda bb, i: (bb * nt + i, 0)
    assert CONV_HALO + t >= 8 * CONV_SEG + CONV_K - 1 and 8 * CONV_SEG >= t
    assert CONV_SEG % CONV_IB == 0 and CONV_SEG % 8 == 4
    slabs = (c // V7X_LANES, t + CONV_HALO, V7X_LANES)
    return pl.pallas_call(
        _conv_kernel,
        grid=(bsz, nt),
        in_specs=[pl.BlockSpec((t, c), row)] + [_const_spec(a.shape) for a in (w, b, lng, lnb)],
        out_specs=pl.BlockSpec((t, c), row),
        out_shape=jax.ShapeDtypeStruct((n, c), BF16),
        scratch_shapes=[pltpu.VMEM(slabs, F32), pltpu.VMEM(slabs, F32)],
        compiler_params=_cparams(24 << 20),
        name="conv",
    )(x, w, b, lng, lnb)


def _merge_kernel(h_ref, ya_ref, yb_ref, yc_ref, yd_ref, g1_ref, wg_ref, bg_ref, wb_ref, wo_ref, o_ref):
    h = h_ref[...]
    xn = _rms(h, g1_ref[...]).astype(BF16)
    merged = None
    for n, y_ref in enumerate((ya_ref, yb_ref, yc_ref, yd_ref)):
        gate = _sigmoid(_dot(xn, wg_ref[n]) + bg_ref[n])
        term = gate * _dot(y_ref[...], wb_ref[n])
        merged = term if merged is None else merged + term
    o_ref[...] = h + _dot(merged.astype(BF16), wo_ref[...])


def _merge(h, ys, g1, wg, bg, wb, wo):
    n, d = h.shape
    t = TOK_TILE
    row = lambda i: (i, 0)
    consts = (g1, wg, bg, wb, wo)
    return pl.pallas_call(
        _merge_kernel,
        grid=(n // t,),
        in_specs=[pl.BlockSpec((t, d), row)] + [pl.BlockSpec((t, y.shape[1]), row) for y in ys]
                 + [_const_spec(c.shape) for c in consts],
        out_specs=pl.BlockSpec((t, d), row),
        out_shape=jax.ShapeDtypeStruct((n, d), F32),
        compiler_params=_cparams(48 << 20),
        name="merge",
    )(h, *ys, *consts)


def _ffn_ple_kernel(h_ref, p_ref, g2_ref, w1_ref, w2_ref, g3_ref, wpg_ref, bpg_ref, wp_ref, gf_ref,
                    o_ref, *, final_norm, ff_chunk):
    h = h_ref[...]
    hn = _rms(h, g2_ref[...]).astype(BF16)
    acc = h
    for j in range(w1_ref.shape[1] // ff_chunk):
        cols = slice(j * ff_chunk, (j + 1) * ff_chunk)
        a = jnp.maximum(_dot(hn, w1_ref[:, cols]), 0.0)
        acc = acc + _dot((a * a).astype(BF16), w2_ref[cols, :])
    hg = _rms(acc, g3_ref[...]).astype(BF16)
    gate = _sigmoid(_dot(hg, wpg_ref[...]) + bpg_ref[...])
    out = acc + gate * _dot(p_ref[...].astype(BF16), wp_ref[...])
    if final_norm:
        out = _rms(out, gf_ref[...])
    o_ref[...] = out


def _ffn_ple(h, p, g2, w1, w2, g3, wpg, bpg, wp, gf, final_norm):
    n, d = h.shape
    t = TOK_TILE
    row = lambda i: (i, 0)
    consts = (g2, w1, w2, g3, wpg, bpg, wp, gf)
    return pl.pallas_call(
        functools.partial(_ffn_ple_kernel, final_norm=final_norm, ff_chunk=1024),
        grid=(n // t,),
        in_specs=[pl.BlockSpec((t, d), row), pl.BlockSpec((t, p.shape[1]), row)]
                 + [_const_spec(c.shape) for c in consts],
        out_specs=pl.BlockSpec((t, d), row),
        out_shape=jax.ShapeDtypeStruct((n, d), F32),
        compiler_params=_cparams(52 << 20),
        name="ffn_ple",
    )(h, p, *consts)


def kernel(x, p, norm1_g, w_in, sg_ln_g, sg_ln_b, sg_w, sg_b, gla_w_a2, gla_b_a, gla_norm_g, att_rel_bias, conv_dw_w, conv_dw_b, conv_ln_g, conv_ln_b, w_branch, w_gate, b_gate, w_out, norm2_g, w_ff1, w_ff2, norm3_g, w_ple_gate, b_ple_gate, w_ple, final_g):
    bsz, seq, d = x.shape
    depth = w_in.shape[0]
    n = bsz * seq
    bw = d // 2
    nqk = GLA_HEADS * GLA_DK
    assert seq % TOK_TILE == 0 and seq % ATT_TILE == 0 and bw == SG_GROUPS * V7X_LANES
    row2 = lambda a: a.reshape(1, -1)

    c_sg = 2 * bw
    c_gla = c_sg + 2 * nqk + 2 * bw
    c_ga = c_gla + GLA_RANK
    c_att = c_ga + 3 * bw

    h = x.reshape(n, d)
    for i in range(depth):
        w = w_in[i].astype(BF16)
        wga = jnp.pad(w[:, c_gla:c_ga], ((0, 0), (0, V7X_LANES - GLA_RANK)))
        wa2 = jnp.pad(gla_w_a2[i].astype(BF16), ((0, V7X_LANES - GLA_RANK), (0, 0)))
        sgbias = jnp.repeat(sg_b[i].T, bw // SG_GROUPS, axis=1)
        g1 = row2(norm1_g[i])
        (y_a, gq, gk, gv, gr, la, aq, ak, av, cy) = _in_proj(
            h, g1, w[:, :c_sg], w[:, c_sg:c_gla], wga, w[:, c_ga:c_att], w[:, c_att:],
            row2(sg_ln_g[i]), row2(sg_ln_b[i]), sg_w[i], sgbias, wa2, row2(gla_b_a[i]))
        y_b = _gla(gq, gk, gv, gr, la, row2(gla_norm_g[i]), bsz)
        y_c = _attn(aq, ak, av, _att_bias_table(att_rel_bias[i]), bsz)
        y_d = _conv(cy, jnp.pad(conv_dw_w[i], ((0, 1), (0, 0))), row2(conv_dw_b[i]),
                    row2(conv_ln_g[i]), row2(conv_ln_b[i]), bsz)
        h = _merge(h, (y_a, y_b, y_c, y_d), g1, w_gate[i].astype(BF16), b_gate[i][:, None, :],
                   w_branch[i].astype(BF16), w_out[i].astype(BF16))
        h = _ffn_ple(h, p[i].reshape(n, -1), row2(norm2_g[i]), w_ff1[i].astype(BF16),
                     w_ff2[i].astype(BF16), row2(norm3_g[i]), w_ple_gate[i].astype(BF16),
                     row2(b_ple_gate[i]), w_ple[i].astype(BF16), row2(final_g),
                     final_norm=(i == depth - 1))
    return h.reshape(bsz, seq, d)
```

```python
import functools

import jax
import jax.numpy as jnp
from jax import lax
from jax.experimental import pallas as pl
from jax.experimental.pallas import tpu as pltpu

F32 = jnp.float32
BF16 = jnp.bfloat16

EPS = 1e-6
NEG_INF = -1e30
LOG2E = 1.4426950408889634

CHUNK = 64
SG_BLOCK = 128
SG_GROUPS = 4
GLA_HEADS = 4
GLA_DK = 64
GLA_DV = 128
GLA_RANK = 16
GLA_TAU = 16.0
ATT_HEADS = 8
ATT_HD = 64
ATT_BAND = 9
MAX_REL = 256
CONV_K = 31

V7X_LANES = 128
V7X_VMEM_BYTES = 64 * 1024 * 1024

TOK_TILE = 512
ATT_TILE = 256
ATT_WIN = 3 * ATT_TILE
ATT_GROUP = 4
CONV_HALO = 64
CONV_SEG = 68
CONV_IB = 17
CONV_NB = 64


def _cparams(vmem_bytes):
    return pltpu.CompilerParams(
        dimension_semantics=None,
        vmem_limit_bytes=int(min(vmem_bytes, V7X_VMEM_BYTES - (6 << 20))),
    )


def _const_spec(shape):
    nd = len(shape)
    return pl.BlockSpec(shape, lambda *_: (0,) * nd, pipeline_mode=pl.Buffered(1))


def _rms(x, g):
    ms = jnp.mean(x * x, axis=-1, keepdims=True)
    return x * lax.rsqrt(ms + EPS) * g


def _layer_norm(x, g, b):
    mu = jnp.mean(x, axis=-1, keepdims=True)
    xc = x - mu
    var = jnp.mean(xc * xc, axis=-1, keepdims=True)
    return xc * lax.rsqrt(var + EPS) * g + b


def _dot(a, b):
    return jnp.dot(a, b, preferred_element_type=F32)


def _dot_nt(a, b):
    return lax.dot_general(a, b, (((1,), (1,)), ((), ())), preferred_element_type=F32)


def _dot_tn(a, b):
    return lax.dot_general(a, b, (((0,), (0,)), ((), ())), preferred_element_type=F32)


def _sigmoid(x):
    return 0.5 * jnp.tanh(0.5 * x) + 0.5


def _silu(x):
    return x * _sigmoid(x)


def _log_sigmoid(x):
    return jnp.minimum(x, 0.0) - jnp.log(1.0 + jnp.exp(-jnp.abs(x)))


def _inproj_kernel(h_ref, g1_ref, wsg_ref, wgla_ref, wga_ref, watt_ref, wcv_ref,
                   lng_ref, lnb_ref, sgw_ref, sgbias_ref, wa2_ref, ba_ref,
                   ya_ref, gq_ref, gk_ref, gv_ref, gr_ref, la_ref,
                   aq_ref, ak_ref, av_ref, cy_ref):
    tile = h_ref.shape[0]
    xn = _rms(h_ref[...], g1_ref[...]).astype(BF16)

    ps = _dot(xn, wsg_ref[...])
    half = ps.shape[1] // 2
    u = jax.nn.gelu(ps[:, :half])
    vn = _layer_norm(jax.nn.gelu(ps[:, half:]), lng_ref[...], lnb_ref[...]).astype(BF16)
    pi = lax.broadcasted_iota(jnp.int32, (SG_BLOCK, SG_BLOCK), 0) // CHUNK
    pj = lax.broadcasted_iota(jnp.int32, (SG_BLOCK, SG_BLOCK), 1) // CHUNK
    cg = half // SG_GROUPS
    for g in range(SG_GROUPS):
        wm = jnp.where(pj <= pi, sgw_ref[g], 0.0).astype(BF16)
        cols = slice(g * cg, (g + 1) * cg)
        for n in range(tile // SG_BLOCK):
            rows = slice(n * SG_BLOCK, (n + 1) * SG_BLOCK)
            mixed = _dot(wm, vn[rows, cols]) + sgbias_ref[:, cols]
            ya_ref[rows, cols] = (u[rows, cols] * mixed).astype(ya_ref.dtype)

    pg = _dot(xn, wgla_ref[...])
    nqk = GLA_HEADS * GLA_DK
    nv = GLA_HEADS * GLA_DV
    gq_ref[...] = (pg[:, :nqk] * (GLA_DK ** -0.5)).astype(gq_ref.dtype)
    gk_ref[...] = pg[:, nqk:2 * nqk]
    gv_ref[...] = pg[:, 2 * nqk:2 * nqk + nv].astype(gv_ref.dtype)
    gr_ref[...] = pg[:, 2 * nqk + nv:]
    a_lr = _dot(xn, wga_ref[...]).astype(BF16)
    la_ref[...] = _log_sigmoid(_dot(a_lr, wa2_ref[...]) + ba_ref[...]) / GLA_TAU

    pa = _dot(xn, watt_ref[...])
    na = ATT_HEADS * ATT_HD
    aq_ref[...] = (pa[:, :na] * (ATT_HD ** -0.5 * LOG2E)).astype(aq_ref.dtype)
    ak_ref[...] = pa[:, na:2 * na].astype(ak_ref.dtype)
    av_ref[...] = pa[:, 2 * na:].astype(av_ref.dtype)

    pc = _dot(xn, wcv_ref[...])
    cw = pc.shape[1] // 2
    cy_ref[...] = pc[:, :cw] * _sigmoid(pc[:, cw:])


def _in_proj(h, g1, wsg, wgla, wga, watt, wcv, lng, lnb, sgw, sgbias, wa2, ba):
    n, d = h.shape
    t = TOK_TILE
    bw = wsg.shape[1] // 2
    nqk = GLA_HEADS * GLA_DK
    row = lambda i: (i, 0)
    outs = [
        ((n, bw), BF16),
        ((n, nqk), BF16),
        ((n, nqk), F32),
        ((n, bw), BF16),
        ((n, bw), F32),
        ((n, nqk), F32),
        ((n, bw), BF16),
        ((n, bw), BF16),
        ((n, bw), BF16),
        ((n, bw), F32),
    ]
    consts = (g1, wsg, wgla, wga, watt, wcv, lng, lnb, sgw, sgbias, wa2, ba)
    return pl.pallas_call(
        _inproj_kernel,
        grid=(n // t,),
        in_specs=[pl.BlockSpec((t, d), row)] + [_const_spec(c.shape) for c in consts],
        out_specs=[pl.BlockSpec((t, s[1]), row) for s, _ in outs],
        out_shape=[jax.ShapeDtypeStruct(s, dt) for s, dt in outs],
        compiler_params=_cparams(52 << 20),
        name="in_proj",
    )(h, *consts)


def _gla_kernel(q_ref, k_ref, v_ref, r_ref, la_ref, ng_ref, y_ref,
                st_ref, stb_ref, upd_ref, later_ref, o_ref):
    tile = q_ref.shape[0]
    nchunk = tile // CHUNK
    pair_rows = 2 * GLA_DV
    npair = GLA_HEADS // 2

    @pl.when(pl.program_id(1) == 0)
    def _():
        st_ref[...] = jnp.zeros_like(st_ref)
        ri = lax.broadcasted_iota(jnp.int32, (tile, tile), 0)
        ci = lax.broadcasted_iota(jnp.int32, (tile, tile), 1)
        later_ref[...] = jnp.where((ci > ri) & (ci // CHUNK == ri // CHUNK), 1.0, 0.0).astype(BF16)

    la = la_ref[...]
    hi = la.astype(BF16)
    lo = (la - hi.astype(F32)).astype(BF16)
    sfx = _dot(later_ref[...], hi) + _dot(later_ref[...], lo)
    kd = (k_ref[...] * jnp.exp(sfx)).astype(BF16)
    total = la + sfx

    own_lanes = (lax.broadcasted_iota(jnp.int32, (pair_rows, V7X_LANES), 1) // GLA_DK
                 == lax.broadcasted_iota(jnp.int32, (pair_rows, V7X_LANES), 0) // GLA_DV)

    for c in range(nchunk):
        rows = slice(c * CHUNK, (c + 1) * CHUNK)
        for p in range(npair):
            upd = _dot_tn(v_ref[rows, p * pair_rows:(p + 1) * pair_rows],
                          kd[rows, p * V7X_LANES:(p + 1) * V7X_LANES])
            upd_ref[c, p * pair_rows:(p + 1) * pair_rows, :] = jnp.where(own_lanes, upd, 0.0)

    st = [st_ref[p * pair_rows:(p + 1) * pair_rows, :] for p in range(npair)]
    for c in range(nchunk):
        decay = jnp.exp(total[c * CHUNK:c * CHUNK + 1, :])
        for p in range(npair):
            prow = slice(p * pair_rows, (p + 1) * pair_rows)
            st[p] = st[p] * decay[:, p * V7X_LANES:(p + 1) * V7X_LANES] + upd_ref[c, prow, :]
            stb_ref[c, prow, :] = st[p].astype(BF16)
    for p in range(npair):
        st_ref[p * pair_rows:(p + 1) * pair_rows, :] = st[p]

    for c in range(nchunk):
        rows = slice(c * CHUNK, (c + 1) * CHUNK)
        for p in range(npair):
            prow = slice(p * pair_rows, (p + 1) * pair_rows)
            o_ref[rows, prow] = _dot_nt(q_ref[rows, p * V7X_LANES:(p + 1) * V7X_LANES], stb_ref[c, prow, :])

    for h in range(GLA_HEADS):
        cols = slice(h * GLA_DV, (h + 1) * GLA_DV)
        o = o_ref[:, cols]
        o = o * lax.rsqrt(jnp.mean(o * o, axis=-1, keepdims=True) + EPS) * ng_ref[:, cols]
        y_ref[:, cols] = (o * _silu(r_ref[:, cols])).astype(y_ref.dtype)


def _gla(q, k, v, r, la, ng, bsz):
    n = q.shape[0]
    t = TOK_TILE
    nt = n // bsz // t
    nqk = GLA_HEADS * GLA_DK
    nv = GLA_HEADS * GLA_DV
    row = lambda b, i: (b * nt + i, 0)
    return pl.pallas_call(
        _gla_kernel,
        grid=(bsz, nt),
        in_specs=[pl.BlockSpec((t, nqk), row), pl.BlockSpec((t, nqk), row),
                  pl.BlockSpec((t, nv), row), pl.BlockSpec((t, nv), row),
                  pl.BlockSpec((t, nqk), row), _const_spec(ng.shape)],
        out_specs=pl.BlockSpec((t, nv), row),
        out_shape=jax.ShapeDtypeStruct((n, nv), BF16),
        scratch_shapes=[pltpu.VMEM((nv, V7X_LANES), F32),
                        pltpu.VMEM((t // CHUNK, nv, V7X_LANES), BF16),
                        pltpu.VMEM((t // CHUNK, nv, V7X_LANES), F32),
                        pltpu.VMEM((t, t), BF16), pltpu.VMEM((t, nv), F32)],
        compiler_params=_cparams(32 << 20),
        name="gla",
    )(q, k, v, r, la, ng)


def _attn_kernel(q_ref, k0_ref, k1_ref, k2_ref, v0_ref, v1_ref, v2_ref, bias_ref, y_ref):
    t = pl.program_id(1)
    tq = q_ref.shape[0]
    win = ATT_WIN

    def body(first_tiles):
        krefs = (k0_ref, k1_ref, k2_ref)
        vrefs = (v0_ref, v1_ref, v2_ref)
        key_ok = lax.broadcasted_iota(jnp.int32, (1, win), 1) >= (2 - t) * tq
        head_of_lane = lax.broadcasted_iota(jnp.int32, (1, ATT_GROUP * ATT_HD), 1) // ATT_HD
        for grp in range(ATT_HEADS // ATT_GROUP):
            cols = slice(grp * ATT_GROUP * ATT_HD, (grp + 1) * ATT_GROUP * ATT_HD)
            qg = q_ref[:, cols]
            for hh in range(ATT_GROUP):
                mine = head_of_lane == hh
                qm = jnp.where(mine, qg, jnp.zeros_like(qg))
                s = jnp.concatenate([_dot_nt(qm, kr[:, cols]) for kr in krefs], axis=1)
                s = s + bias_ref[grp * ATT_GROUP + hh]
                if first_tiles:
                    s = jnp.where(key_ok, s, NEG_INF)
                e = jnp.exp2(s - jnp.max(s, axis=-1, keepdims=True))
                r = 1.0 / jnp.sum(e, axis=-1, keepdims=True)
                e = e.astype(BF16)
                pv = sum(_dot(e[:, j * tq:(j + 1) * tq], vr[:, cols]) for j, vr in enumerate(vrefs)) * r
                hd = slice(hh * ATT_HD, (hh + 1) * ATT_HD)
                y_ref[:, cols.start + hd.start:cols.start + hd.stop] = pv[:, hd].astype(y_ref.dtype)

    pl.when(t < 2)(functools.partial(body, True))
    pl.when(t >= 2)(functools.partial(body, False))


def _attn(q, k, v, bias, bsz):
    n, w = q.shape
    t = ATT_TILE
    nt = n // bsz // t
    cur = lambda b, i: (b * nt + i, 0)
    prev1 = lambda b, i: (b * nt + jnp.maximum(i - 1, 0), 0)
    prev2 = lambda b, i: (b * nt + jnp.maximum(i - 2, 0), 0)
    blk = lambda im: pl.BlockSpec((t, w), im)
    return pl.pallas_call(
        _attn_kernel,
        grid=(bsz, nt),
        in_specs=[blk(cur), blk(prev2), blk(prev1), blk(cur), blk(prev2), blk(prev1), blk(cur),
                  _const_spec(bias.shape)],
        out_specs=blk(cur),
        out_shape=jax.ShapeDtypeStruct((n, w), BF16),
        compiler_params=_cparams(40 << 20),
        name="attn",
    )(q, k, k, k, v, v, v, bias)


def _att_bias_table(rel_bias):
    nh, ntab = rel_bias.shape
    period = ATT_TILE + ATT_WIN
    n_lo = (ATT_TILE - 1) - (CHUNK - 1)
    rb = rel_bias.astype(F32)
    vec = jnp.concatenate([jnp.broadcast_to(rb[:, :1], (nh, n_lo)), rb,
                           jnp.broadcast_to(rb[:, -1:], (nh, period - n_lo - ntab))], axis=1)
    rolled = jnp.roll(vec[:, ::-1], -ATT_TILE, axis=1)
    flat = jnp.tile(rolled, (1, ATT_TILE))[:, :ATT_TILE * (period - 1)]
    table = flat.reshape(nh, ATT_TILE, period - 1)[:, :, :ATT_WIN]
    qc = jnp.arange(ATT_TILE)[:, None] // CHUNK + 2 * ATT_TILE // CHUNK
    kc = jnp.arange(ATT_WIN)[None, :] // CHUNK
    in_band = (kc <= qc) & (kc >= qc - (ATT_BAND - 1))
    return jnp.where(in_band[None], table * LOG2E, NEG_INF)


def _conv_tile(x_ref, w_ref, b_ref, lng_ref, lnb_ref, ext_ref, co_ref, y_ref):
    tile, width = x_ref.shape
    nslab = width // V7X_LANES
    out0 = CONV_HALO + tile - 8 * CONV_SEG
    in0 = out0 - (CONV_K - 1)
    lanes = [slice(c * V7X_LANES, (c + 1) * V7X_LANES) for c in range(nslab)]

    @pl.when(pl.program_id(1) == 0)
    def _():
        ext_ref[:, 0:CONV_HALO, :] = jnp.zeros((nslab, CONV_HALO, V7X_LANES), F32)

    for c in range(nslab):
        ext_ref[c, CONV_HALO:, :] = x_ref[:, lanes[c]]

    for c in range(nslab):
        for i0 in range(0, CONV_SEG, CONV_IB):
            accs = [None] * CONV_IB
            for j in range(CONV_K):
                wj = jnp.broadcast_to(w_ref[j:j + 1, lanes[c]], (8, V7X_LANES))
                for a in range(CONV_IB):
                    term = wj * ext_ref[c, pl.ds(in0 + i0 + a + j, 8, stride=CONV_SEG), :]
                    accs[a] = term if accs[a] is None else accs[a] + term
            for a in range(CONV_IB):
                co_ref[c, pl.ds(out0 + i0 + a, 8, stride=CONV_SEG), :] = accs[a]

    for r0 in range(0, tile, CONV_NB):
        parts = [co_ref[c, CONV_HALO + r0:CONV_HALO + r0 + CONV_NB, :] + b_ref[:, lanes[c]]
                 for c in range(nslab)]
        mu = jnp.sum(sum(parts), axis=-1, keepdims=True) / width
        cen = [q - mu for q in parts]
        var = jnp.sum(sum(q * q for q in cen), axis=-1, keepdims=True) / width
        inv = lax.rsqrt(var + EPS)
        for c in range(nslab):
            y = cen[c] * inv * lng_ref[:, lanes[c]] + lnb_ref[:, lanes[c]]
            y_ref[r0:r0 + CONV_NB, lanes[c]] = _silu(y).astype(y_ref.dtype)

    for c in range(nslab):
        ext_ref[c, 0:CONV_HALO, :] = ext_ref[c, tile:tile + CONV_HALO, :]


def _merge_kernel(h_ref, ya_ref, yb_ref, yc_ref, cy_ref, g1_ref, wg_ref, bg_ref, wb_ref, wo_ref,
                  cw_ref, cb_ref, clg_ref, clb_ref, o_ref, ext_ref, co_ref, yd_ref):
    h = h_ref[...]
    xn = _rms(h, g1_ref[...]).astype(BF16)
    _conv_tile(cy_ref, cw_ref, cb_ref, clg_ref, clb_ref, ext_ref, co_ref, yd_ref)
    merged = None
    for n, y_ref in enumerate((ya_ref, yb_ref, yc_ref, yd_ref)):
        gate = _sigmoid(_dot(xn, wg_ref[n]) + bg_ref[n])
        term = gate * _dot(y_ref[...], wb_ref[n])
        merged = term if merged is None else merged + term
    o_ref[...] = h + _dot(merged.astype(BF16), wo_ref[...])


def _merge(h, ys, cy, g1, wg, bg, wb, wo, cw, cb, clg, clb, bsz):
    n, d = h.shape
    t = TOK_TILE
    nt = n // bsz // t
    cwid = cy.shape[1]
    assert CONV_HALO + t >= 8 * CONV_SEG + CONV_K - 1 and 8 * CONV_SEG >= t
    assert CONV_SEG % CONV_IB == 0 and CONV_SEG % 8 == 4
    row = lambda b, i: (b * nt + i, 0)
    consts = (g1, wg, bg, wb, wo, cw, cb, clg, clb)
    slabs = (cwid // V7X_LANES, t + CONV_HALO, V7X_LANES)
    return pl.pallas_call(
        _merge_kernel,
        grid=(bsz, nt),
        in_specs=[pl.BlockSpec((t, d), row)] + [pl.BlockSpec((t, y.shape[1]), row) for y in ys]
                 + [pl.BlockSpec((t, cwid), row)] + [_const_spec(c.shape) for c in consts],
        out_specs=pl.BlockSpec((t, d), row),
        out_shape=jax.ShapeDtypeStruct((n, d), F32),
        scratch_shapes=[pltpu.VMEM(slabs, F32), pltpu.VMEM(slabs, F32), pltpu.VMEM((t, cwid), BF16)],
        compiler_params=_cparams(52 << 20),
        name="merge",
    )(h, *ys, cy, *consts)


def _ffn_ple_kernel(h_ref, p_ref, g2_ref, w1_ref, w2_ref, g3_ref, wpg_ref, bpg_ref, wp_ref, gf_ref,
                    o_ref, *, final_norm, ff_chunk):
    h = h_ref[...]
    hn = _rms(h, g2_ref[...]).astype(BF16)
    acc = h
    for j in range(w1_ref.shape[1] // ff_chunk):
        cols = slice(j * ff_chunk, (j + 1) * ff_chunk)
        a = jnp.maximum(_dot(hn, w1_ref[:, cols]), 0.0)
        acc = acc + _dot((a * a).astype(BF16), w2_ref[cols, :])
    hg = _rms(acc, g3_ref[...]).astype(BF16)
    gate = _sigmoid(_dot(hg, wpg_ref[...]) + bpg_ref[...])
    out = acc + gate * _dot(p_ref[...].astype(BF16), wp_ref[...])
    if final_norm:
        out = _rms(out, gf_ref[...])
    o_ref[...] = out


def _ffn_ple(h, p, g2, w1, w2, g3, wpg, bpg, wp, gf, final_norm):
    n, d = h.shape
    t = TOK_TILE
    row = lambda i: (i, 0)
    consts = (g2, w1, w2, g3, wpg, bpg, wp, gf)
    return pl.pallas_call(
        functools.partial(_ffn_ple_kernel, final_norm=final_norm, ff_chunk=1024),
        grid=(n // t,),
        in_specs=[pl.BlockSpec((t, d), row), pl.BlockSpec((t, p.shape[1]), row)]
                 + [_const_spec(c.shape) for c in consts],
        out_specs=pl.BlockSpec((t, d), row),
        out_shape=jax.ShapeDtypeStruct((n, d), F32),
        compiler_params=_cparams(52 << 20),
        name="ffn_ple",
    )(h, p, *consts)


def kernel(x, p, norm1_g, w_in, sg_ln_g, sg_ln_b, sg_w, sg_b, gla_w_a2, gla_b_a, gla_norm_g, att_rel_bias, conv_dw_w, conv_dw_b, conv_ln_g, conv_ln_b, w_branch, w_gate, b_gate, w_out, norm2_g, w_ff1, w_ff2, norm3_g, w_ple_gate, b_ple_gate, w_ple, final_g):
    bsz, seq, d = x.shape
    depth = w_in.shape[0]
    n = bsz * seq
    bw = d // 2
    nqk = GLA_HEADS * GLA_DK
    assert seq % TOK_TILE == 0 and seq % ATT_TILE == 0 and bw == SG_GROUPS * V7X_LANES
    row2 = lambda a: a.reshape(1, -1)

    c_sg = 2 * bw
    c_gla = c_sg + 2 * nqk + 2 * bw
    c_ga = c_gla + GLA_RANK
    c_att = c_ga + 3 * bw

    h = x.reshape(n, d)
    for i in range(depth):
        w = w_in[i].astype(BF16)
        wga = jnp.pad(w[:, c_gla:c_ga], ((0, 0), (0, V7X_LANES - GLA_RANK)))
        wa2 = jnp.pad(gla_w_a2[i].astype(BF16), ((0, V7X_LANES - GLA_RANK), (0, 0)))
        sgbias = jnp.repeat(sg_b[i].T, bw // SG_GROUPS, axis=1)
        g1 = row2(norm1_g[i])
        (y_a, gq, gk, gv, gr, la, aq, ak, av, cy) = _in_proj(
            h, g1, w[:, :c_sg], w[:, c_sg:c_gla], wga, w[:, c_ga:c_att], w[:, c_att:],
            row2(sg_ln_g[i]), row2(sg_ln_b[i]), sg_w[i], sgbias, wa2, row2(gla_b_a[i]))
        y_b = _gla(gq, gk, gv, gr, la, row2(gla_norm_g[i]), bsz)
        y_c = _attn(aq, ak, av, _att_bias_table(att_rel_bias[i]), bsz)
        h = _merge(h, (y_a, y_b, y_c), cy, g1, w_gate[i].astype(BF16), b_gate[i][:, None, :],
                   w_branch[i].astype(BF16), w_out[i].astype(BF16),
                   jnp.pad(conv_dw_w[i], ((0, 1), (0, 0))), row2(conv_dw_b[i]),
                   row2(conv_ln_g[i]), row2(conv_ln_b[i]), bsz)
        h = _ffn_ple(h, p[i].reshape(n, -1), row2(norm2_g[i]), w_ff1[i].astype(BF16),
                     w_ff2[i].astype(BF16), row2(norm3_g[i]), w_ple_gate[i].astype(BF16),
                     row2(b_ple_gate[i]), w_ple[i].astype(BF16), row2(final_g),
                     final_norm=(i == depth - 1))
    return h.reshape(bsz, seq, d)
```

```python
import functools

import jax
import jax.numpy as jnp
from jax import lax
from jax.experimental import pallas as pl
from jax.experimental.pallas import tpu as pltpu

F32 = jnp.float32
BF16 = jnp.bfloat16

EPS = 1e-6
NEG_INF = -1e30
LOG2E = 1.4426950408889634

CHUNK = 64
SG_BLOCK = 128
SG_GROUPS = 4
GLA_HEADS = 4
GLA_DK = 64
GLA_DV = 128
GLA_RANK = 16
GLA_TAU = 16.0
ATT_HEADS = 8
ATT_HD = 64
ATT_BAND = 9
MAX_REL = 256
CONV_K = 31

V7X_LANES = 128
V7X_VMEM_BYTES = 64 * 1024 * 1024

TOK_TILE = 512
DENSE_TILE = 1024
ATT_TILE = 256
ATT_WIN = 3 * ATT_TILE
ATT_GROUP = 4
CONV_HALO = 64
CONV_SEG = 68
CONV_IB = 17
CONV_NB = 64


def _cparams(vmem_bytes):
    return pltpu.CompilerParams(
        dimension_semantics=None,
        vmem_limit_bytes=int(min(vmem_bytes, V7X_VMEM_BYTES - (6 << 20))),
    )


def _const_spec(shape):
    nd = len(shape)
    return pl.BlockSpec(shape, lambda *_: (0,) * nd, pipeline_mode=pl.Buffered(1))


def _rms(x, g):
    ms = jnp.mean(x * x, axis=-1, keepdims=True)
    return x * lax.rsqrt(ms + EPS) * g


def _layer_norm(x, g, b):
    mu = jnp.mean(x, axis=-1, keepdims=True)
    xc = x - mu
    var = jnp.mean(xc * xc, axis=-1, keepdims=True)
    return xc * lax.rsqrt(var + EPS) * g + b


def _dot(a, b):
    return jnp.dot(a, b, preferred_element_type=F32)


def _dot_nt(a, b):
    return lax.dot_general(a, b, (((1,), (1,)), ((), ())), preferred_element_type=F32)


def _dot_tn(a, b):
    return lax.dot_general(a, b, (((0,), (0,)), ((), ())), preferred_element_type=F32)


def _sigmoid(x):
    return 0.5 * jnp.tanh(0.5 * x) + 0.5


def _silu(x):
    return x * _sigmoid(x)


def _log_sigmoid(x):
    return jnp.minimum(x, 0.0) - jnp.log(1.0 + jnp.exp(-jnp.abs(x)))


def _inproj_kernel(h_ref, g1_ref, wsg_ref, wgla_ref, wga_ref, watt_ref, wcv_ref,
                   lng_ref, lnb_ref, sgw_ref, sgbias_ref, wa2_ref, ba_ref,
                   cw_ref, cb_ref, clg_ref, clb_ref,
                   ya_ref, gq_ref, gk_ref, gv_ref, gr_ref, la_ref,
                   aq_ref, ak_ref, av_ref, yd_ref, cy_ref, ext_ref, co_ref):
    tile = h_ref.shape[0]
    xn = _rms(h_ref[...], g1_ref[...]).astype(BF16)

    pc = _dot(xn, wcv_ref[...])
    cw = pc.shape[1] // 2
    cy_ref[...] = pc[:, :cw] * _sigmoid(pc[:, cw:])
    _conv_tile(cy_ref, cw_ref, cb_ref, clg_ref, clb_ref, ext_ref, co_ref, yd_ref)

    ps = _dot(xn, wsg_ref[...])
    half = ps.shape[1] // 2
    u = jax.nn.gelu(ps[:, :half])
    vn = _layer_norm(jax.nn.gelu(ps[:, half:]), lng_ref[...], lnb_ref[...]).astype(BF16)
    pi = lax.broadcasted_iota(jnp.int32, (SG_BLOCK, SG_BLOCK), 0) // CHUNK
    pj = lax.broadcasted_iota(jnp.int32, (SG_BLOCK, SG_BLOCK), 1) // CHUNK
    cg = half // SG_GROUPS
    nblk = tile // SG_BLOCK
    for g in range(SG_GROUPS):
        wm = jnp.where(pj <= pi, sgw_ref[g], 0.0).astype(BF16)
        cols = slice(g * cg, (g + 1) * cg)
        vblocks = jnp.concatenate([vn[n * SG_BLOCK:(n + 1) * SG_BLOCK, cols] for n in range(nblk)], axis=1)
        mixed = _dot(wm, vblocks)
        for n in range(nblk):
            rows = slice(n * SG_BLOCK, (n + 1) * SG_BLOCK)
            ya_ref[rows, cols] = (u[rows, cols] * (mixed[:, n * cg:(n + 1) * cg] + sgbias_ref[:, cols])
                                  ).astype(ya_ref.dtype)

    pg = _dot(xn, wgla_ref[...])
    nqk = GLA_HEADS * GLA_DK
    nv = GLA_HEADS * GLA_DV
    gq_ref[...] = (pg[:, :nqk] * (GLA_DK ** -0.5)).astype(gq_ref.dtype)
    gk_ref[...] = pg[:, nqk:2 * nqk]
    gv_ref[...] = pg[:, 2 * nqk:2 * nqk + nv].astype(gv_ref.dtype)
    gr_ref[...] = pg[:, 2 * nqk + nv:]
    a_lr = _dot(xn, wga_ref[...]).astype(BF16)
    la_ref[...] = _log_sigmoid(_dot(a_lr, wa2_ref[...]) + ba_ref[...]) / GLA_TAU

    pa = _dot(xn, watt_ref[...])
    na = ATT_HEADS * ATT_HD
    aq_ref[...] = (pa[:, :na] * (ATT_HD ** -0.5 * LOG2E)).astype(aq_ref.dtype)
    ak_ref[...] = pa[:, na:2 * na].astype(ak_ref.dtype)
    av_ref[...] = pa[:, 2 * na:].astype(av_ref.dtype)


def _in_proj(h, g1, wsg, wgla, wga, watt, wcv, lng, lnb, sgw, sgbias, wa2, ba, cw, cb, clg, clb, bsz):
    n, d = h.shape
    t = TOK_TILE
    nt = n // bsz // t
    bw = wsg.shape[1] // 2
    nqk = GLA_HEADS * GLA_DK
    assert CONV_HALO + t >= 8 * CONV_SEG + CONV_K - 1 and 8 * CONV_SEG >= t
    assert CONV_SEG % CONV_IB == 0 and CONV_SEG % 8 == 4
    row = lambda b, i: (b * nt + i, 0)
    slabs = (bw // V7X_LANES, t + CONV_HALO, V7X_LANES)
    outs = [
        ((n, bw), BF16),
        ((n, nqk), BF16),
        ((n, nqk), F32),
        ((n, bw), BF16),
        ((n, bw), F32),
        ((n, nqk), F32),
        ((n, bw), BF16),
        ((n, bw), BF16),
        ((n, bw), BF16),
        ((n, bw), BF16),
    ]
    consts = (g1, wsg, wgla, wga, watt, wcv, lng, lnb, sgw, sgbias, wa2, ba, cw, cb, clg, clb)
    return pl.pallas_call(
        _inproj_kernel,
        grid=(bsz, nt),
        in_specs=[pl.BlockSpec((t, d), row)] + [_const_spec(c.shape) for c in consts],
        out_specs=[pl.BlockSpec((t, s[1]), row) for s, _ in outs],
        out_shape=[jax.ShapeDtypeStruct(s, dt) for s, dt in outs],
        scratch_shapes=[pltpu.VMEM((t, bw), F32), pltpu.VMEM(slabs, F32), pltpu.VMEM(slabs, F32)],
        compiler_params=_cparams(56 << 20),
        name="in_proj",
    )(h, *consts)


def _gla_kernel(q_ref, k_ref, v_ref, r_ref, la_ref, ng_ref, y_ref,
                st_ref, stb_ref, upd_ref, later_ref, o_ref):
    tile = q_ref.shape[0]
    nchunk = tile // CHUNK
    pair_rows = 2 * GLA_DV
    npair = GLA_HEADS // 2

    @pl.when(pl.program_id(1) == 0)
    def _():
        st_ref[...] = jnp.zeros_like(st_ref)
        ri = lax.broadcasted_iota(jnp.int32, (tile, tile), 0)
        ci = lax.broadcasted_iota(jnp.int32, (tile, tile), 1)
        later_ref[...] = jnp.where((ci > ri) & (ci // CHUNK == ri // CHUNK), 1.0, 0.0).astype(BF16)

    la = la_ref[...]
    hi = la.astype(BF16)
    lo = (la - hi.astype(F32)).astype(BF16)
    sfx = _dot(later_ref[...], hi) + _dot(later_ref[...], lo)
    kd = (k_ref[...] * jnp.exp(sfx)).astype(BF16)
    total = la + sfx

    own_lanes = (lax.broadcasted_iota(jnp.int32, (pair_rows, V7X_LANES), 1) // GLA_DK
                 == lax.broadcasted_iota(jnp.int32, (pair_rows, V7X_LANES), 0) // GLA_DV)

    for c in range(nchunk):
        rows = slice(c * CHUNK, (c + 1) * CHUNK)
        for p in range(npair):
            upd = _dot_tn(v_ref[rows, p * pair_rows:(p + 1) * pair_rows],
                          kd[rows, p * V7X_LANES:(p + 1) * V7X_LANES])
            upd_ref[c, p * pair_rows:(p + 1) * pair_rows, :] = jnp.where(own_lanes, upd, 0.0)

    st = [st_ref[p * pair_rows:(p + 1) * pair_rows, :] for p in range(npair)]
    for c in range(nchunk):
        decay = jnp.exp(total[c * CHUNK:c * CHUNK + 1, :])
        for p in range(npair):
            prow = slice(p * pair_rows, (p + 1) * pair_rows)
            st[p] = st[p] * decay[:, p * V7X_LANES:(p + 1) * V7X_LANES] + upd_ref[c, prow, :]
            stb_ref[c, prow, :] = st[p].astype(BF16)
    for p in range(npair):
        st_ref[p * pair_rows:(p + 1) * pair_rows, :] = st[p]

    for c in range(nchunk):
        rows = slice(c * CHUNK, (c + 1) * CHUNK)
        for p in range(npair):
            prow = slice(p * pair_rows, (p + 1) * pair_rows)
            o_ref[rows, prow] = _dot_nt(q_ref[rows, p * V7X_LANES:(p + 1) * V7X_LANES], stb_ref[c, prow, :])

    for h in range(GLA_HEADS):
        cols = slice(h * GLA_DV, (h + 1) * GLA_DV)
        o = o_ref[:, cols]
        o = o * lax.rsqrt(jnp.mean(o * o, axis=-1, keepdims=True) + EPS) * ng_ref[:, cols]
        y_ref[:, cols] = (o * _silu(r_ref[:, cols])).astype(y_ref.dtype)


def _gla(q, k, v, r, la, ng, bsz):
    n = q.shape[0]
    t = TOK_TILE
    nt = n // bsz // t
    nqk = GLA_HEADS * GLA_DK
    nv = GLA_HEADS * GLA_DV
    row = lambda b, i: (b * nt + i, 0)
    return pl.pallas_call(
        _gla_kernel,
        grid=(bsz, nt),
        in_specs=[pl.BlockSpec((t, nqk), row), pl.BlockSpec((t, nqk), row),
                  pl.BlockSpec((t, nv), row), pl.BlockSpec((t, nv), row),
                  pl.BlockSpec((t, nqk), row), _const_spec(ng.shape)],
        out_specs=pl.BlockSpec((t, nv), row),
        out_shape=jax.ShapeDtypeStruct((n, nv), BF16),
        scratch_shapes=[pltpu.VMEM((nv, V7X_LANES), F32),
                        pltpu.VMEM((t // CHUNK, nv, V7X_LANES), BF16),
                        pltpu.VMEM((t // CHUNK, nv, V7X_LANES), F32),
                        pltpu.VMEM((t, t), BF16), pltpu.VMEM((t, nv), F32)],
        compiler_params=_cparams(32 << 20),
        name="gla",
    )(q, k, v, r, la, ng)


def _attn_kernel(q_ref, k0_ref, k1_ref, k2_ref, v0_ref, v1_ref, v2_ref, bias_ref, y_ref):
    t = pl.program_id(1)
    tq = q_ref.shape[0]
    win = ATT_WIN

    def body(first_tiles):
        krefs = (k0_ref, k1_ref, k2_ref)
        vrefs = (v0_ref, v1_ref, v2_ref)
        key_ok = lax.broadcasted_iota(jnp.int32, (1, win), 1) >= (2 - t) * tq
        head_of_lane = lax.broadcasted_iota(jnp.int32, (1, ATT_GROUP * ATT_HD), 1) // ATT_HD
        for grp in range(ATT_HEADS // ATT_GROUP):
            cols = slice(grp * ATT_GROUP * ATT_HD, (grp + 1) * ATT_GROUP * ATT_HD)
            qg = q_ref[:, cols]
            for hh in range(ATT_GROUP):
                mine = head_of_lane == hh
                qm = jnp.where(mine, qg, jnp.zeros_like(qg))
                s = jnp.concatenate([_dot_nt(qm, kr[:, cols]) for kr in krefs], axis=1)
                s = s + bias_ref[grp * ATT_GROUP + hh]
                if first_tiles:
                    s = jnp.where(key_ok, s, NEG_INF)
                e = jnp.exp2(s - jnp.max(s, axis=-1, keepdims=True))
                r = 1.0 / jnp.sum(e, axis=-1, keepdims=True)
                e = e.astype(BF16)
                pv = sum(_dot(e[:, j * tq:(j + 1) * tq], vr[:, cols]) for j, vr in enumerate(vrefs)) * r
                hd = slice(hh * ATT_HD, (hh + 1) * ATT_HD)
                y_ref[:, cols.start + hd.start:cols.start + hd.stop] = pv[:, hd].astype(y_ref.dtype)

    pl.when(t < 2)(functools.partial(body, True))
    pl.when(t >= 2)(functools.partial(body, False))


def _attn(q, k, v, bias, bsz):
    n, w = q.shape
    t = ATT_TILE
    nt = n // bsz // t
    cur = lambda b, i: (b * nt + i, 0)
    prev1 = lambda b, i: (b * nt + jnp.maximum(i - 1, 0), 0)
    prev2 = lambda b, i: (b * nt + jnp.maximum(i - 2, 0), 0)
    blk = lambda im: pl.BlockSpec((t, w), im)
    return pl.pallas_call(
        _attn_kernel,
        grid=(bsz, nt),
        in_specs=[blk(cur), blk(prev2), blk(prev1), blk(cur), blk(prev2), blk(prev1), blk(cur),
                  _const_spec(bias.shape)],
        out_specs=blk(cur),
        out_shape=jax.ShapeDtypeStruct((n, w), BF16),
        compiler_params=_cparams(40 << 20),
        name="attn",
    )(q, k, k, k, v, v, v, bias)


def _att_bias_table(rel_bias):
    nh, ntab = rel_bias.shape
    period = ATT_TILE + ATT_WIN
    n_lo = (ATT_TILE - 1) - (CHUNK - 1)
    rb = rel_bias.astype(F32)
    vec = jnp.concatenate([jnp.broadcast_to(rb[:, :1], (nh, n_lo)), rb,
                           jnp.broadcast_to(rb[:, -1:], (nh, period - n_lo - ntab))], axis=1)
    rolled = jnp.roll(vec[:, ::-1], -ATT_TILE, axis=1)
    flat = jnp.tile(rolled, (1, ATT_TILE))[:, :ATT_TILE * (period - 1)]
    table = flat.reshape(nh, ATT_TILE, period - 1)[:, :, :ATT_WIN]
    qc = jnp.arange(ATT_TILE)[:, None] // CHUNK + 2 * ATT_TILE // CHUNK
    kc = jnp.arange(ATT_WIN)[None, :] // CHUNK
    in_band = (kc <= qc) & (kc >= qc - (ATT_BAND - 1))
    return jnp.where(in_band[None], table * LOG2E, NEG_INF)


def _conv_tile(x_ref, w_ref, b_ref, lng_ref, lnb_ref, ext_ref, co_ref, y_ref):
    tile, width = x_ref.shape
    nslab = width // V7X_LANES
    out0 = CONV_HALO + tile - 8 * CONV_SEG
    in0 = out0 - (CONV_K - 1)
    lanes = [slice(c * V7X_LANES, (c + 1) * V7X_LANES) for c in range(nslab)]

    @pl.when(pl.program_id(1) == 0)
    def _():
        ext_ref[:, 0:CONV_HALO, :] = jnp.zeros((nslab, CONV_HALO, V7X_LANES), F32)

    for c in range(nslab):
        ext_ref[c, CONV_HALO:, :] = x_ref[:, lanes[c]]

    for c in range(nslab):
        for i0 in range(0, CONV_SEG, CONV_IB):
            accs = [None] * CONV_IB
            for j in range(CONV_K):
                wj = jnp.broadcast_to(w_ref[j:j + 1, lanes[c]], (8, V7X_LANES))
                for a in range(CONV_IB):
                    term = wj * ext_ref[c, pl.ds(in0 + i0 + a + j, 8, stride=CONV_SEG), :]
                    accs[a] = term if accs[a] is None else accs[a] + term
            for a in range(CONV_IB):
                co_ref[c, pl.ds(out0 + i0 + a, 8, stride=CONV_SEG), :] = accs[a]

    for r0 in range(0, tile, CONV_NB):
        parts = [co_ref[c, CONV_HALO + r0:CONV_HALO + r0 + CONV_NB, :] + b_ref[:, lanes[c]]
                 for c in range(nslab)]
        mu = jnp.sum(sum(parts), axis=-1, keepdims=True) / width
        cen = [q - mu for q in parts]
        var = jnp.sum(sum(q * q for q in cen), axis=-1, keepdims=True) / width
        inv = lax.rsqrt(var + EPS)
        for c in range(nslab):
            y = cen[c] * inv * lng_ref[:, lanes[c]] + lnb_ref[:, lanes[c]]
            y_ref[r0:r0 + CONV_NB, lanes[c]] = _silu(y).astype(y_ref.dtype)

    for c in range(nslab):
        ext_ref[c, 0:CONV_HALO, :] = ext_ref[c, tile:tile + CONV_HALO, :]


def _merge_kernel(h_ref, ya_ref, yb_ref, yc_ref, yd_ref, g1_ref, wg_ref, bg_ref, wb_ref, wo_ref, o_ref):
    h = h_ref[...]
    xn = _rms(h, g1_ref[...]).astype(BF16)
    merged = None
    for n, y_ref in enumerate((ya_ref, yb_ref, yc_ref, yd_ref)):
        gate = _sigmoid(_dot(xn, wg_ref[n]) + bg_ref[n])
        term = gate * _dot(y_ref[...], wb_ref[n])
        merged = term if merged is None else merged + term
    o_ref[...] = h + _dot(merged.astype(BF16), wo_ref[...])


def _merge(h, ys, g1, wg, bg, wb, wo):
    n, d = h.shape
    t = DENSE_TILE
    row = lambda i: (i, 0)
    consts = (g1, wg, bg, wb, wo)
    return pl.pallas_call(
        _merge_kernel,
        grid=(n // t,),
        in_specs=[pl.BlockSpec((t, d), row)] + [pl.BlockSpec((t, y.shape[1]), row) for y in ys]
                 + [_const_spec(c.shape) for c in consts],
        out_specs=pl.BlockSpec((t, d), row),
        out_shape=jax.ShapeDtypeStruct((n, d), F32),
        compiler_params=_cparams(58 << 20),
        name="merge",
    )(h, *ys, *consts)


def _ffn_ple_kernel(h_ref, p_ref, g2_ref, w1_ref, w2_ref, g3_ref, wpg_ref, bpg_ref, wp_ref, gf_ref,
                    o_ref, *, final_norm, ff_chunk):
    h = h_ref[...]
    hn = _rms(h, g2_ref[...]).astype(BF16)
    acc = h
    for j in range(w1_ref.shape[1] // ff_chunk):
        cols = slice(j * ff_chunk, (j + 1) * ff_chunk)
        a = jnp.maximum(_dot(hn, w1_ref[:, cols]), 0.0)
        acc = acc + _dot((a * a).astype(BF16), w2_ref[cols, :])
    hg = _rms(acc, g3_ref[...]).astype(BF16)
    gate = _sigmoid(_dot(hg, wpg_ref[...]) + bpg_ref[...])
    out = acc + gate * _dot(p_ref[...].astype(BF16), wp_ref[...])
    if final_norm:
        out = _rms(out, gf_ref[...])
    o_ref[...] = out


def _ffn_ple(h, p, layer, g2, w1, w2, g3, wpg, bpg, wp, gf, final_norm):
    n, d = h.shape
    t = DENSE_TILE
    row = lambda i: (i, 0)
    consts = (g2, w1, w2, g3, wpg, bpg, wp, gf)
    return pl.pallas_call(
        functools.partial(_ffn_ple_kernel, final_norm=final_norm, ff_chunk=1024),
        grid=(n // t,),
        in_specs=[pl.BlockSpec((t, d), row),
                  pl.BlockSpec((None, t, p.shape[2]), lambda i: (layer, i, 0))]
                 + [_const_spec(c.shape) for c in consts],
        out_specs=pl.BlockSpec((t, d), row),
        out_shape=jax.ShapeDtypeStruct((n, d), F32),
        compiler_params=_cparams(58 << 20),
        name="ffn_ple",
    )(h, p, *consts)


def kernel(x, p, norm1_g, w_in, sg_ln_g, sg_ln_b, sg_w, sg_b, gla_w_a2, gla_b_a, gla_norm_g, att_rel_bias, conv_dw_w, conv_dw_b, conv_ln_g, conv_ln_b, w_branch, w_gate, b_gate, w_out, norm2_g, w_ff1, w_ff2, norm3_g, w_ple_gate, b_ple_gate, w_ple, final_g):
    bsz, seq, d = x.shape
    depth = w_in.shape[0]
    n = bsz * seq
    bw = d // 2
    nqk = GLA_HEADS * GLA_DK
    assert seq % TOK_TILE == 0 and seq % ATT_TILE == 0 and n % DENSE_TILE == 0
    assert bw == SG_GROUPS * V7X_LANES
    row2 = lambda a: a.reshape(1, -1)

    c_sg = 2 * bw
    c_gla = c_sg + 2 * nqk + 2 * bw
    c_ga = c_gla + GLA_RANK
    c_att = c_ga + 3 * bw

    h = x.reshape(n, d)
    p3 = p.reshape(depth, n, -1)
    for i in range(depth):
        wcols = lambda a, b: w_in[i, :, a:b].astype(BF16)
        wga = jnp.pad(wcols(c_gla, c_ga), ((0, 0), (0, V7X_LANES - GLA_RANK)))
        wa2 = jnp.pad(gla_w_a2[i].astype(BF16), ((0, V7X_LANES - GLA_RANK), (0, 0)))
        sgbias = jnp.repeat(sg_b[i].T, bw // SG_GROUPS, axis=1)
        g1 = row2(norm1_g[i])
        (y_a, gq, gk, gv, gr, la, aq, ak, av, y_d) = _in_proj(
            h, g1, wcols(0, c_sg), wcols(c_sg, c_gla), wga, wcols(c_ga, c_att), wcols(c_att, w_in.shape[2]),
            row2(sg_ln_g[i]), row2(sg_ln_b[i]), sg_w[i], sgbias, wa2, row2(gla_b_a[i]),
            jnp.pad(conv_dw_w[i], ((0, 1), (0, 0))), row2(conv_dw_b[i]),
            row2(conv_ln_g[i]), row2(conv_ln_b[i]), bsz)
        y_b = _gla(gq, gk, gv, gr, la, row2(gla_norm_g[i]), bsz)
        y_c = _attn(aq, ak, av, _att_bias_table(att_rel_bias[i]), bsz)
        h = _merge(h, (y_a, y_b, y_c, y_d), g1, w_gate[i].astype(BF16), b_gate[i][:, None, :],
                   w_branch[i].astype(BF16), w_out[i].astype(BF16))
        h = _ffn_ple(h, p3, i, row2(norm2_g[i]), w_ff1[i].astype(BF16),
                     w_ff2[i].astype(BF16), row2(norm3_g[i]), w_ple_gate[i].astype(BF16),
                     row2(b_ple_gate[i]), w_ple[i].astype(BF16), row2(final_g),
                     final_norm=(i == depth - 1))
    return h.reshape(bsz, seq, d)
```

```python
import functools

import jax
import jax.numpy as jnp
from jax import lax
from jax.experimental import pallas as pl
from jax.experimental.pallas import tpu as pltpu

F32 = jnp.float32
BF16 = jnp.bfloat16

EPS = 1e-6
NEG_INF = -1e30
LOG2E = 1.4426950408889634

CHUNK = 64
SG_BLOCK = 128
SG_GROUPS = 4
GLA_HEADS = 4
GLA_DK = 64
GLA_DV = 128
GLA_RANK = 16
GLA_TAU = 16.0
ATT_HEADS = 8
ATT_HD = 64
ATT_BAND = 9
MAX_REL = 256
CONV_K = 31

V7X_LANES = 128
V7X_VMEM_BYTES = 64 * 1024 * 1024

TOK_TILE = 512
DENSE_TILE = 1024
ATT_TILE = 256
ATT_WIN = 3 * ATT_TILE
ATT_GROUP = 4
ATT_SUB = 2
CONV_HALO = 64
CONV_SEG = 68
CONV_IB = 17
CONV_NB = 64


def _cparams(vmem_bytes, flags=None):
    return pltpu.CompilerParams(
        dimension_semantics=None,
        vmem_limit_bytes=int(min(vmem_bytes, V7X_VMEM_BYTES - (6 << 20))),
        flags=flags,
    )


def _const_spec(shape):
    nd = len(shape)
    return pl.BlockSpec(shape, lambda *_: (0,) * nd, pipeline_mode=pl.Buffered(1))


def _rms(x, g):
    ms = jnp.mean(x * x, axis=-1, keepdims=True)
    return x * lax.rsqrt(ms + EPS) * g


def _layer_norm(x, g, b):
    mu = jnp.mean(x, axis=-1, keepdims=True)
    xc = x - mu
    var = jnp.mean(xc * xc, axis=-1, keepdims=True)
    return xc * lax.rsqrt(var + EPS) * g + b


def _dot(a, b):
    return jnp.dot(a, b, preferred_element_type=F32)


def _dot_nt(a, b):
    return lax.dot_general(a, b, (((1,), (1,)), ((), ())), preferred_element_type=F32)


def _dot_tn(a, b):
    return lax.dot_general(a, b, (((0,), (0,)), ((), ())), preferred_element_type=F32)


def _sigmoid(x):
    return 0.5 * jnp.tanh(0.5 * x) + 0.5


def _silu(x):
    return x * _sigmoid(x)


def _log_sigmoid(x):
    return jnp.minimum(x, 0.0) - jnp.log(1.0 + jnp.exp(-jnp.abs(x)))


def _inproj_kernel(h_ref, g1_ref, w_ref,
                   lng_ref, lnb_ref, sgw_ref, sgbias_ref, wa2_ref, ba_ref,
                   cw_ref, cb_ref, clg_ref, clb_ref,
                   ya_ref, gq_ref, gk_ref, gv_ref, gr_ref, la_ref,
                   aq_ref, ak_ref, av_ref, yd_ref, cy_ref, ext_ref, co_ref):
    tile = h_ref.shape[0]
    xn = _rms(h_ref[...], g1_ref[...]).astype(BF16)
    bw = ya_ref.shape[1]
    o_gla = 2 * bw
    o_att = o_gla + 2 * GLA_HEADS * GLA_DK + 2 * bw
    o_cv = o_att + 3 * bw
    o_ga = o_cv + 2 * bw

    pc = _dot(xn, w_ref[:, o_cv:o_ga])
    cw = pc.shape[1] // 2
    cy_ref[...] = pc[:, :cw] * _sigmoid(pc[:, cw:])
    _conv_tile(cy_ref, cw_ref, cb_ref, clg_ref, clb_ref, ext_ref, co_ref, yd_ref)

    ps = _dot(xn, w_ref[:, 0:o_gla])
    half = ps.shape[1] // 2
    u = jax.nn.gelu(ps[:, :half])
    vn = _layer_norm(jax.nn.gelu(ps[:, half:]), lng_ref[...], lnb_ref[...]).astype(BF16)
    pi = lax.broadcasted_iota(jnp.int32, (SG_BLOCK, SG_BLOCK), 0) // CHUNK
    pj = lax.broadcasted_iota(jnp.int32, (SG_BLOCK, SG_BLOCK), 1) // CHUNK
    cg = half // SG_GROUPS
    nblk = tile // SG_BLOCK
    for g in range(SG_GROUPS):
        wm = jnp.where(pj <= pi, sgw_ref[g], 0.0).astype(BF16)
        cols = slice(g * cg, (g + 1) * cg)
        vblocks = jnp.concatenate([vn[n * SG_BLOCK:(n + 1) * SG_BLOCK, cols] for n in range(nblk)], axis=1)
        mixed = _dot(wm, vblocks)
        for n in range(nblk):
            rows = slice(n * SG_BLOCK, (n + 1) * SG_BLOCK)
            ya_ref[rows, cols] = (u[rows, cols] * (mixed[:, n * cg:(n + 1) * cg] + sgbias_ref[:, cols])
                                  ).astype(ya_ref.dtype)

    pg = _dot(xn, w_ref[:, o_gla:o_att])
    nqk = GLA_HEADS * GLA_DK
    nv = GLA_HEADS * GLA_DV
    gq_ref[...] = (pg[:, :nqk] * (GLA_DK ** -0.5)).astype(gq_ref.dtype)
    gk_ref[...] = pg[:, nqk:2 * nqk]
    gv_ref[...] = pg[:, 2 * nqk:2 * nqk + nv].astype(gv_ref.dtype)
    gr_ref[...] = pg[:, 2 * nqk + nv:]
    a_lr = _dot(xn, w_ref[:, o_ga:]).astype(BF16)
    la_ref[...] = _log_sigmoid(_dot(a_lr, wa2_ref[...]) + ba_ref[...]) / GLA_TAU

    pa = _dot(xn, w_ref[:, o_att:o_cv])
    na = ATT_HEADS * ATT_HD
    aq_ref[...] = (pa[:, :na] * (ATT_HD ** -0.5 * LOG2E)).astype(aq_ref.dtype)
    ak_ref[...] = pa[:, na:2 * na].astype(ak_ref.dtype)
    av_ref[...] = pa[:, 2 * na:].astype(av_ref.dtype)


def _in_proj(h, g1, w, lng, lnb, sgw, sgbias, wa2, ba, cw, cb, clg, clb, bsz):
    n, d = h.shape
    t = TOK_TILE
    nt = n // bsz // t
    bw = d // 2
    nqk = GLA_HEADS * GLA_DK
    assert CONV_HALO + t >= 8 * CONV_SEG + CONV_K - 1 and 8 * CONV_SEG >= t
    assert CONV_SEG % CONV_IB == 0 and CONV_SEG % 8 == 4
    row = lambda b, i: (b * nt + i, 0)
    slabs = (bw // V7X_LANES, t + CONV_HALO, V7X_LANES)
    outs = [
        ((n, bw), BF16),
        ((n, nqk), BF16),
        ((n, nqk), F32),
        ((n, bw), BF16),
        ((n, bw), F32),
        ((n, nqk), F32),
        ((n, bw), BF16),
        ((n, bw), BF16),
        ((n, bw), BF16),
        ((n, bw), BF16),
    ]
    consts = (g1, w, lng, lnb, sgw, sgbias, wa2, ba, cw, cb, clg, clb)
    return pl.pallas_call(
        _inproj_kernel,
        grid=(bsz, nt),
        in_specs=[pl.BlockSpec((t, d), row)] + [_const_spec(c.shape) for c in consts],
        out_specs=[pl.BlockSpec((t, s[1]), row) for s, _ in outs],
        out_shape=[jax.ShapeDtypeStruct(s, dt) for s, dt in outs],
        scratch_shapes=[pltpu.VMEM((t, bw), F32), pltpu.VMEM(slabs, F32), pltpu.VMEM(slabs, F32)],
        compiler_params=_cparams(56 << 20),
        name="in_proj",
    )(h, *consts)


def _gla_kernel(q_ref, k_ref, v_ref, r_ref, la_ref, ng_ref, y_ref,
                st_ref, stb_ref, upd_ref, later_ref, o_ref):
    tile = q_ref.shape[0]
    nchunk = tile // CHUNK
    pair_rows = 2 * GLA_DV
    npair = GLA_HEADS // 2

    @pl.when(pl.program_id(1) == 0)
    def _():
        st_ref[...] = jnp.zeros_like(st_ref)
        ri = lax.broadcasted_iota(jnp.int32, (tile, tile), 0)
        ci = lax.broadcasted_iota(jnp.int32, (tile, tile), 1)
        later_ref[...] = jnp.where((ci > ri) & (ci // CHUNK == ri // CHUNK), 1.0, 0.0).astype(BF16)

    la = la_ref[...]
    hi = la.astype(BF16)
    lo = (la - hi.astype(F32)).astype(BF16)
    sfx = _dot(later_ref[...], hi) + _dot(later_ref[...], lo)
    kd = (k_ref[...] * jnp.exp(sfx)).astype(BF16)
    total = la + sfx

    own_lanes = (lax.broadcasted_iota(jnp.int32, (pair_rows, V7X_LANES), 1) // GLA_DK
                 == lax.broadcasted_iota(jnp.int32, (pair_rows, V7X_LANES), 0) // GLA_DV)

    for c in range(nchunk):
        rows = slice(c * CHUNK, (c + 1) * CHUNK)
        for p in range(npair):
            upd = _dot_tn(v_ref[rows, p * pair_rows:(p + 1) * pair_rows],
                          kd[rows, p * V7X_LANES:(p + 1) * V7X_LANES])
            upd_ref[c, p * pair_rows:(p + 1) * pair_rows, :] = jnp.where(own_lanes, upd, 0.0)

    st = [st_ref[p * pair_rows:(p + 1) * pair_rows, :] for p in range(npair)]
    for c in range(nchunk):
        decay = jnp.exp(total[c * CHUNK:c * CHUNK + 1, :])
        for p in range(npair):
            prow = slice(p * pair_rows, (p + 1) * pair_rows)
            st[p] = st[p] * decay[:, p * V7X_LANES:(p + 1) * V7X_LANES] + upd_ref[c, prow, :]
            stb_ref[c, prow, :] = st[p].astype(BF16)
    for p in range(npair):
        st_ref[p * pair_rows:(p + 1) * pair_rows, :] = st[p]

    for c in range(nchunk):
        rows = slice(c * CHUNK, (c + 1) * CHUNK)
        for p in range(npair):
            prow = slice(p * pair_rows, (p + 1) * pair_rows)
            o_ref[rows, prow] = _dot_nt(q_ref[rows, p * V7X_LANES:(p + 1) * V7X_LANES], stb_ref[c, prow, :])

    for h in range(GLA_HEADS):
        cols = slice(h * GLA_DV, (h + 1) * GLA_DV)
        o = o_ref[:, cols]
        o = o * lax.rsqrt(jnp.mean(o * o, axis=-1, keepdims=True) + EPS) * ng_ref[:, cols]
        y_ref[:, cols] = (o * _silu(r_ref[:, cols])).astype(y_ref.dtype)


def _gla(q, k, v, r, la, ng, bsz):
    n = q.shape[0]
    t = TOK_TILE
    nt = n // bsz // t
    nqk = GLA_HEADS * GLA_DK
    nv = GLA_HEADS * GLA_DV
    row = lambda b, i: (b * nt + i, 0)
    return pl.pallas_call(
        _gla_kernel,
        grid=(bsz, nt),
        in_specs=[pl.BlockSpec((t, nqk), row), pl.BlockSpec((t, nqk), row),
                  pl.BlockSpec((t, nv), row), pl.BlockSpec((t, nv), row),
                  pl.BlockSpec((t, nqk), row), _const_spec(ng.shape)],
        out_specs=pl.BlockSpec((t, nv), row),
        out_shape=jax.ShapeDtypeStruct((n, nv), BF16),
        scratch_shapes=[pltpu.VMEM((nv, V7X_LANES), F32),
                        pltpu.VMEM((t // CHUNK, nv, V7X_LANES), BF16),
                        pltpu.VMEM((t // CHUNK, nv, V7X_LANES), F32),
                        pltpu.VMEM((t, t), BF16), pltpu.VMEM((t, nv), F32)],
        compiler_params=_cparams(32 << 20),
        name="gla",
    )(q, k, v, r, la, ng)


def _attn_kernel(q_ref, kp_ref, kc_ref, vp_ref, vc_ref, bias_ref, y_ref, s_ref, e_ref):
    t = pl.program_id(1)
    tq = ATT_TILE

    def body(first_step):
        head_of_lane = lax.broadcasted_iota(jnp.int32, (1, ATT_GROUP * ATT_HD), 1) // ATT_HD
        for sub in range(ATT_SUB):
            pieces = []
            for back in (2, 1, 0):
                r0 = (sub - back) * tq
                pieces.append((kp_ref, vp_ref, r0 + ATT_SUB * tq, True) if r0 < 0
                              else (kc_ref, vc_ref, r0, False))
            qrows = slice(sub * tq, (sub + 1) * tq)
            for grp in range(ATT_HEADS // ATT_GROUP):
                cols = slice(grp * ATT_GROUP * ATT_HD, (grp + 1) * ATT_GROUP * ATT_HD)
                qg = q_ref[qrows, cols]
                for hh in range(ATT_GROUP):
                    mine = head_of_lane == hh
                    qm = jnp.where(mine, qg, jnp.zeros_like(qg))
                    head = grp * ATT_GROUP + hh
                    for j, (kr, _, r0, from_prev) in enumerate(pieces):
                        kc = slice(j * tq, (j + 1) * tq)
                        if first_step and from_prev:
                            s_ref[head, :, kc] = jnp.full((tq, tq), NEG_INF, F32)
                        else:
                            s_ref[head, :, kc] = _dot_nt(qm, kr[r0:r0 + tq, cols]) + bias_ref[head, :, kc]
                    e = jnp.exp2(s_ref[head] - jnp.max(s_ref[head], axis=-1, keepdims=True))
                    r = 1.0 / jnp.sum(e, axis=-1, keepdims=True)
                    e_ref[head] = e.astype(BF16)
                    pv = sum(_dot(e_ref[head, :, j * tq:(j + 1) * tq], vr[r0:r0 + tq, cols])
                             for j, (_, vr, r0, _) in enumerate(pieces)) * r
                    hd = slice(hh * ATT_HD, (hh + 1) * ATT_HD)
                    y_ref[qrows, cols.start + hd.start:cols.start + hd.stop] = pv[:, hd].astype(y_ref.dtype)

    pl.when(t == 0)(functools.partial(body, True))
    pl.when(t > 0)(functools.partial(body, False))


def _attn(q, k, v, bias, bsz):
    n, w = q.shape
    t = ATT_SUB * ATT_TILE
    nt = n // bsz // t
    cur = lambda b, i: (b * nt + i, 0)
    prev = lambda b, i: (b * nt + jnp.maximum(i - 1, 0), 0)
    blk = lambda im: pl.BlockSpec((t, w), im)
    return pl.pallas_call(
        _attn_kernel,
        grid=(bsz, nt),
        in_specs=[blk(cur), blk(prev), blk(cur), blk(prev), blk(cur), _const_spec(bias.shape)],
        out_specs=blk(cur),
        out_shape=jax.ShapeDtypeStruct((n, w), BF16),
        scratch_shapes=[pltpu.VMEM((ATT_HEADS, ATT_TILE, ATT_WIN), F32),
                        pltpu.VMEM((ATT_HEADS, ATT_TILE, ATT_WIN), BF16)],
        compiler_params=_cparams(48 << 20),
        name="attn",
    )(q, k, k, v, v, bias)


def _att_bias_table(rel_bias):
    nh, ntab = rel_bias.shape
    period = ATT_TILE + ATT_WIN
    n_lo = (ATT_TILE - 1) - (CHUNK - 1)
    rb = rel_bias.astype(F32)
    vec = jnp.concatenate([jnp.broadcast_to(rb[:, :1], (nh, n_lo)), rb,
                           jnp.broadcast_to(rb[:, -1:], (nh, period - n_lo - ntab))], axis=1)
    rolled = jnp.roll(vec[:, ::-1], -ATT_TILE, axis=1)
    flat = jnp.tile(rolled, (1, ATT_TILE))[:, :ATT_TILE * (period - 1)]
    table = flat.reshape(nh, ATT_TILE, period - 1)[:, :, :ATT_WIN]
    qc = jnp.arange(ATT_TILE)[:, None] // CHUNK + 2 * ATT_TILE // CHUNK
    kc = jnp.arange(ATT_WIN)[None, :] // CHUNK
    in_band = (kc <= qc) & (kc >= qc - (ATT_BAND - 1))
    return jnp.where(in_band[None], table * LOG2E, NEG_INF)


def _conv_tile(x_ref, w_ref, b_ref, lng_ref, lnb_ref, ext_ref, co_ref, y_ref):
    tile, width = x_ref.shape
    nslab = width // V7X_LANES
    out0 = CONV_HALO + tile - 8 * CONV_SEG
    in0 = out0 - (CONV_K - 1)
    lanes = [slice(c * V7X_LANES, (c + 1) * V7X_LANES) for c in range(nslab)]

    @pl.when(pl.program_id(1) == 0)
    def _():
        ext_ref[:, 0:CONV_HALO, :] = jnp.zeros((nslab, CONV_HALO, V7X_LANES), F32)

    for c in range(nslab):
        ext_ref[c, CONV_HALO:, :] = x_ref[:, lanes[c]]

    for c in range(nslab):
        for i0 in range(0, CONV_SEG, CONV_IB):
            accs = [None] * CONV_IB
            for j in range(CONV_K):
                wj = jnp.broadcast_to(w_ref[j:j + 1, lanes[c]], (8, V7X_LANES))
                for a in range(CONV_IB):
                    term = wj * ext_ref[c, pl.ds(in0 + i0 + a + j, 8, stride=CONV_SEG), :]
                    accs[a] = term if accs[a] is None else accs[a] + term
            for a in range(CONV_IB):
                co_ref[c, pl.ds(out0 + i0 + a, 8, stride=CONV_SEG), :] = accs[a]

    for r0 in range(0, tile, CONV_NB):
        parts = [co_ref[c, CONV_HALO + r0:CONV_HALO + r0 + CONV_NB, :] + b_ref[:, lanes[c]]
                 for c in range(nslab)]
        mu = jnp.sum(sum(parts), axis=-1, keepdims=True) / width
        cen = [q - mu for q in parts]
        var = jnp.sum(sum(q * q for q in cen), axis=-1, keepdims=True) / width
        inv = lax.rsqrt(var + EPS)
        for c in range(nslab):
            y = cen[c] * inv * lng_ref[:, lanes[c]] + lnb_ref[:, lanes[c]]
            y_ref[r0:r0 + CONV_NB, lanes[c]] = _silu(y).astype(y_ref.dtype)

    for c in range(nslab):
        ext_ref[c, 0:CONV_HALO, :] = ext_ref[c, tile:tile + CONV_HALO, :]


def _merge_kernel(h_ref, ya_ref, yb_ref, yc_ref, yd_ref, g1_ref, wg_ref, bg_ref, wb_ref, wo_ref, o_ref):
    h = h_ref[...]
    xn = _rms(h, g1_ref[...]).astype(BF16)
    merged = None
    for n, y_ref in enumerate((ya_ref, yb_ref, yc_ref, yd_ref)):
        gate = _sigmoid(_dot(xn, wg_ref[n]) + bg_ref[n])
        term = gate * _dot(y_ref[...], wb_ref[n])
        merged = term if merged is None else merged + term
    o_ref[...] = h + _dot(merged.astype(BF16), wo_ref[...])


def _merge(h, ys, g1, wg, bg, wb, wo):
    n, d = h.shape
    t = DENSE_TILE
    row = lambda i: (i, 0)
    consts = (g1, wg, bg, wb, wo)
    return pl.pallas_call(
        _merge_kernel,
        grid=(n // t,),
        in_specs=[pl.BlockSpec((t, d), row)] + [pl.BlockSpec((t, y.shape[1]), row) for y in ys]
                 + [_const_spec(c.shape) for c in consts],
        out_specs=pl.BlockSpec((t, d), row),
        out_shape=jax.ShapeDtypeStruct((n, d), F32),
        compiler_params=_cparams(58 << 20),
        name="merge",
    )(h, *ys, *consts)


def _ffn_ple_kernel(h_ref, p_ref, g2_ref, w1_ref, w2_ref, g3_ref, wpg_ref, bpg_ref, wp_ref, gf_ref,
                    o_ref, *, final_norm, ff_chunk):
    h = h_ref[...]
    hn = _rms(h, g2_ref[...]).astype(BF16)
    acc = h
    for j in range(w1_ref.shape[1] // ff_chunk):
        cols = slice(j * ff_chunk, (j + 1) * ff_chunk)
        a = jnp.maximum(_dot(hn, w1_ref[:, cols]), 0.0)
        acc = acc + _dot((a * a).astype(BF16), w2_ref[cols, :])
    hg = _rms(acc, g3_ref[...]).astype(BF16)
    gate = _sigmoid(_dot(hg, wpg_ref[...]) + bpg_ref[...])
    out = acc + gate * _dot(p_ref[...].astype(BF16), wp_ref[...])
    if final_norm:
        out = _rms(out, gf_ref[...])
    o_ref[...] = out


def _ffn_ple(h, p, layer, g2, w1, w2, g3, wpg, bpg, wp, gf, final_norm):
    n, d = h.shape
    t = DENSE_TILE
    row = lambda i: (i, 0)
    consts = (g2, w1, w2, g3, wpg, bpg, wp, gf)
    return pl.pallas_call(
        functools.partial(_ffn_ple_kernel, final_norm=final_norm, ff_chunk=1024),
        grid=(n // t,),
        in_specs=[pl.BlockSpec((t, d), row),
                  pl.BlockSpec((None, t, p.shape[2]), lambda i: (layer, i, 0))]
                 + [_const_spec(c.shape) for c in consts],
        out_specs=pl.BlockSpec((t, d), row),
        out_shape=jax.ShapeDtypeStruct((n, d), F32),
        compiler_params=_cparams(58 << 20),
        name="ffn_ple",
    )(h, p, *consts)


def kernel(x, p, norm1_g, w_in, sg_ln_g, sg_ln_b, sg_w, sg_b, gla_w_a2, gla_b_a, gla_norm_g, att_rel_bias, conv_dw_w, conv_dw_b, conv_ln_g, conv_ln_b, w_branch, w_gate, b_gate, w_out, norm2_g, w_ff1, w_ff2, norm3_g, w_ple_gate, b_ple_gate, w_ple, final_g):
    bsz, seq, d = x.shape
    depth = w_in.shape[0]
    n = bsz * seq
    bw = d // 2
    nqk = GLA_HEADS * GLA_DK
    assert seq % TOK_TILE == 0 and seq % (ATT_SUB * ATT_TILE) == 0 and n % DENSE_TILE == 0
    assert bw == SG_GROUPS * V7X_LANES
    row2 = lambda a: a.reshape(1, -1)

    c_ga = 2 * bw + 2 * nqk + 2 * bw
    c_att = c_ga + GLA_RANK

    h = x.reshape(n, d)
    p3 = p.reshape(depth, n, -1)
    for i in range(depth):
        w = jnp.concatenate([w_in[i, :, :c_ga], w_in[i, :, c_att:],
                             jnp.pad(w_in[i, :, c_ga:c_att], ((0, 0), (0, V7X_LANES - GLA_RANK)))],
                            axis=1).astype(BF16)
        wa2 = jnp.pad(gla_w_a2[i].astype(BF16), ((0, V7X_LANES - GLA_RANK), (0, 0)))
        sgbias = jnp.repeat(sg_b[i].T, bw // SG_GROUPS, axis=1)
        g1 = row2(norm1_g[i])
        (y_a, gq, gk, gv, gr, la, aq, ak, av, y_d) = _in_proj(
            h, g1, w, row2(sg_ln_g[i]), row2(sg_ln_b[i]), sg_w[i], sgbias, wa2, row2(gla_b_a[i]),
            jnp.pad(conv_dw_w[i], ((0, 1), (0, 0))), row2(conv_dw_b[i]),
            row2(conv_ln_g[i]), row2(conv_ln_b[i]), bsz)
        y_b = _gla(gq, gk, gv, gr, la, row2(gla_norm_g[i]), bsz)
        y_c = _attn(aq, ak, av, _att_bias_table(att_rel_bias[i]), bsz)
        h = _merge(h, (y_a, y_b, y_c, y_d), g1, w_gate[i].astype(BF16), b_gate[i][:, None, :],
                   w_branch[i].astype(BF16), w_out[i].astype(BF16))
        h = _ffn_ple(h, p3, i, row2(norm2_g[i]), w_ff1[i].astype(BF16),
                     w_ff2[i].astype(BF16), row2(norm3_g[i]), w_ple_gate[i].astype(BF16),
                     row2(b_ple_gate[i]), w_ple[i].astype(BF16), row2(final_g),
                     final_norm=(i == depth - 1))
    return h.reshape(bsz, seq, d)
```

```python
import functools

import jax
import jax.numpy as jnp
from jax import lax
from jax.experimental import pallas as pl
from jax.experimental.pallas import tpu as pltpu

F32 = jnp.float32
BF16 = jnp.bfloat16

EPS = 1e-6
NEG_INF = -1e30
LOG2E = 1.4426950408889634

CHUNK = 64
SG_BLOCK = 128
SG_GROUPS = 4
GLA_HEADS = 4
GLA_DK = 64
GLA_DV = 128
GLA_RANK = 16
GLA_TAU = 16.0
ATT_HEADS = 8
ATT_HD = 64
ATT_BAND = 9
MAX_REL = 256
CONV_K = 31

V7X_LANES = 128
BF16_SUBLANES = 16
V7X_VMEM_BYTES = 64 * 1024 * 1024

TOK_TILE = 512
DENSE_TILE = 1024
ATT_TILE = 256
ATT_WIN = 3 * ATT_TILE
ATT_GROUP = 4
ATT_SUB = 2
CONV_HALO = 64
CONV_SEG = 68
CONV_IB = 17
CONV_NB = 64


def _cparams(vmem_bytes, flags=None):
    return pltpu.CompilerParams(
        dimension_semantics=None,
        vmem_limit_bytes=int(min(vmem_bytes, V7X_VMEM_BYTES - (6 << 20))),
        flags=flags,
    )


def _const_spec(shape):
    nd = len(shape)
    return pl.BlockSpec(shape, lambda *_: (0,) * nd, pipeline_mode=pl.Buffered(1))


def _rms(x, g):
    ms = jnp.mean(x * x, axis=-1, keepdims=True)
    return x * lax.rsqrt(ms + EPS) * g


def _layer_norm(x, g, b):
    mu = jnp.mean(x, axis=-1, keepdims=True)
    xc = x - mu
    var = jnp.mean(xc * xc, axis=-1, keepdims=True)
    return xc * lax.rsqrt(var + EPS) * g + b


def _dot(a, b):
    return jnp.dot(a, b, preferred_element_type=F32)


def _dot_nt(a, b):
    return lax.dot_general(a, b, (((1,), (1,)), ((), ())), preferred_element_type=F32)


def _dot_tn(a, b):
    return lax.dot_general(a, b, (((0,), (0,)), ((), ())), preferred_element_type=F32)


def _sigmoid(x):
    return 0.5 * jnp.tanh(0.5 * x) + 0.5


def _silu(x):
    return x * _sigmoid(x)


def _log_sigmoid(x):
    return jnp.minimum(x, 0.0) - jnp.log(1.0 + jnp.exp(-jnp.abs(x)))


def _inproj_kernel(h_ref, g1_ref, w_ref,
                   lng_ref, lnb_ref, sgw_ref, sgbias_ref, wa2_ref, ba_ref,
                   cw_ref, cb_ref, clg_ref, clb_ref,
                   ya_ref, gq_ref, gk_ref, gv_ref, gr_ref, la_ref,
                   aq_ref, ak_ref, av_ref, yd_ref, cy_ref, ext_ref, co_ref):
    tile = h_ref.shape[0]
    xn = _rms(h_ref[...], g1_ref[...]).astype(BF16)
    bw = ya_ref.shape[1]
    o_gla = 2 * bw
    o_att = o_gla + 2 * GLA_HEADS * GLA_DK + 2 * bw
    o_cv = o_att + 3 * bw
    o_ga = o_cv + 2 * bw

    pc = _dot(xn, w_ref[:, o_cv:o_ga])
    cw = pc.shape[1] // 2
    cy_ref[...] = pc[:, :cw] * _sigmoid(pc[:, cw:])
    _conv_tile(cy_ref, cw_ref, cb_ref, clg_ref, clb_ref, ext_ref, co_ref, yd_ref)

    ps = _dot(xn, w_ref[:, 0:o_gla])
    half = ps.shape[1] // 2
    u = jax.nn.gelu(ps[:, :half])
    vn = _layer_norm(jax.nn.gelu(ps[:, half:]), lng_ref[...], lnb_ref[...]).astype(BF16)
    pi = lax.broadcasted_iota(jnp.int32, (SG_BLOCK, SG_BLOCK), 0) // CHUNK
    pj = lax.broadcasted_iota(jnp.int32, (SG_BLOCK, SG_BLOCK), 1) // CHUNK
    cg = half // SG_GROUPS
    nblk = tile // SG_BLOCK
    for g in range(SG_GROUPS):
        wm = jnp.where(pj <= pi, sgw_ref[g], 0.0).astype(BF16)
        cols = slice(g * cg, (g + 1) * cg)
        vblocks = jnp.concatenate([vn[n * SG_BLOCK:(n + 1) * SG_BLOCK, cols] for n in range(nblk)], axis=1)
        mixed = _dot(wm, vblocks)
        for n in range(nblk):
            rows = slice(n * SG_BLOCK, (n + 1) * SG_BLOCK)
            ya_ref[rows, cols] = (u[rows, cols] * (mixed[:, n * cg:(n + 1) * cg] + sgbias_ref[:, cols])
                                  ).astype(ya_ref.dtype)

    pg = _dot(xn, w_ref[:, o_gla:o_att])
    nqk = GLA_HEADS * GLA_DK
    nv = GLA_HEADS * GLA_DV
    gq_ref[...] = (pg[:, :nqk] * (GLA_DK ** -0.5)).astype(gq_ref.dtype)
    gk_ref[...] = pg[:, nqk:2 * nqk]
    gv_ref[...] = pg[:, 2 * nqk:2 * nqk + nv].astype(gv_ref.dtype)
    gr_ref[...] = pg[:, 2 * nqk + nv:]
    a_lr = _dot(xn, w_ref[:, o_ga:]).astype(BF16)
    la_ref[...] = _log_sigmoid(_dot(a_lr, wa2_ref[...]) + ba_ref[...]) / GLA_TAU

    pa = _dot(xn, w_ref[:, o_att:o_cv])
    na = ATT_HEADS * ATT_HD
    aq_ref[...] = (pa[:, :na] * (ATT_HD ** -0.5 * LOG2E)).astype(aq_ref.dtype)
    ak_ref[...] = pa[:, na:2 * na].astype(ak_ref.dtype)
    av_ref[...] = pa[:, 2 * na:].astype(av_ref.dtype)


def _in_proj(h, g1, w, lng, lnb, sgw, sgbias, wa2, ba, cw, cb, clg, clb, bsz):
    n, d = h.shape
    t = TOK_TILE
    nt = n // bsz // t
    bw = d // 2
    nqk = GLA_HEADS * GLA_DK
    assert CONV_HALO + t >= 8 * CONV_SEG + CONV_K - 1 and 8 * CONV_SEG >= t
    assert CONV_SEG % CONV_IB == 0 and CONV_SEG % 8 == 4
    row = lambda b, i: (b * nt + i, 0)
    slabs = (bw // V7X_LANES, t + CONV_HALO, V7X_LANES)
    outs = [
        ((n, bw), BF16),
        ((n, nqk), BF16),
        ((n, nqk), F32),
        ((n, bw), BF16),
        ((n, bw), F32),
        ((n, nqk), F32),
        ((n, bw), BF16),
        ((n, bw), BF16),
        ((n, bw), BF16),
        ((n, bw), BF16),
    ]
    consts = (g1, w, lng, lnb, sgw, sgbias, wa2, ba, cw, cb, clg, clb)
    return pl.pallas_call(
        _inproj_kernel,
        grid=(bsz, nt),
        in_specs=[pl.BlockSpec((t, d), row)] + [_const_spec(c.shape) for c in consts],
        out_specs=[pl.BlockSpec((t, s[1]), row) for s, _ in outs],
        out_shape=[jax.ShapeDtypeStruct(s, dt) for s, dt in outs],
        scratch_shapes=[pltpu.VMEM((t, bw), F32), pltpu.VMEM(slabs, F32), pltpu.VMEM(slabs, F32)],
        compiler_params=_cparams(56 << 20),
        name="in_proj",
    )(h, *consts)


def _gla_kernel(q_ref, k_ref, v_ref, r_ref, la_ref, ng_ref, y_ref,
                st_ref, stb_ref, upd_ref, later_ref, o_ref):
    tile = q_ref.shape[0]
    nchunk = tile // CHUNK
    pair_rows = 2 * GLA_DV
    npair = GLA_HEADS // 2

    @pl.when(pl.program_id(1) == 0)
    def _():
        st_ref[...] = jnp.zeros_like(st_ref)
        ri = lax.broadcasted_iota(jnp.int32, (tile, tile), 0)
        ci = lax.broadcasted_iota(jnp.int32, (tile, tile), 1)
        later_ref[...] = jnp.where((ci > ri) & (ci // CHUNK == ri // CHUNK), 1.0, 0.0).astype(BF16)

    la = la_ref[...]
    hi = la.astype(BF16)
    lo = (la - hi.astype(F32)).astype(BF16)
    sfx = _dot(later_ref[...], hi) + _dot(later_ref[...], lo)
    kd = (k_ref[...] * jnp.exp(sfx)).astype(BF16)
    total = la + sfx

    own_lanes = (lax.broadcasted_iota(jnp.int32, (pair_rows, V7X_LANES), 1) // GLA_DK
                 == lax.broadcasted_iota(jnp.int32, (pair_rows, V7X_LANES), 0) // GLA_DV)

    for c in range(nchunk):
        rows = slice(c * CHUNK, (c + 1) * CHUNK)
        for p in range(npair):
            upd = _dot_tn(v_ref[rows, p * pair_rows:(p + 1) * pair_rows],
                          kd[rows, p * V7X_LANES:(p + 1) * V7X_LANES])
            upd_ref[c, p * pair_rows:(p + 1) * pair_rows, :] = jnp.where(own_lanes, upd, 0.0)

    st = [st_ref[p * pair_rows:(p + 1) * pair_rows, :] for p in range(npair)]
    for c in range(nchunk):
        decay = jnp.exp(total[c * CHUNK:c * CHUNK + 1, :])
        for p in range(npair):
            prow = slice(p * pair_rows, (p + 1) * pair_rows)
            st[p] = st[p] * decay[:, p * V7X_LANES:(p + 1) * V7X_LANES] + upd_ref[c, prow, :]
            stb_ref[c, prow, :] = st[p].astype(BF16)
    for p in range(npair):
        st_ref[p * pair_rows:(p + 1) * pair_rows, :] = st[p]

    for c in range(nchunk):
        rows = slice(c * CHUNK, (c + 1) * CHUNK)
        for p in range(npair):
            prow = slice(p * pair_rows, (p + 1) * pair_rows)
            o_ref[rows, prow] = _dot_nt(q_ref[rows, p * V7X_LANES:(p + 1) * V7X_LANES], stb_ref[c, prow, :])

    for h in range(GLA_HEADS):
        cols = slice(h * GLA_DV, (h + 1) * GLA_DV)
        o = o_ref[:, cols]
        o = o * lax.rsqrt(jnp.mean(o * o, axis=-1, keepdims=True) + EPS) * ng_ref[:, cols]
        y_ref[:, cols] = (o * _silu(r_ref[:, cols])).astype(y_ref.dtype)


def _gla(q, k, v, r, la, ng, bsz):
    n = q.shape[0]
    t = TOK_TILE
    nt = n // bsz // t
    nqk = GLA_HEADS * GLA_DK
    nv = GLA_HEADS * GLA_DV
    row = lambda b, i: (b * nt + i, 0)
    return pl.pallas_call(
        _gla_kernel,
        grid=(bsz, nt),
        in_specs=[pl.BlockSpec((t, nqk), row), pl.BlockSpec((t, nqk), row),
                  pl.BlockSpec((t, nv), row), pl.BlockSpec((t, nv), row),
                  pl.BlockSpec((t, nqk), row), _const_spec(ng.shape)],
        out_specs=pl.BlockSpec((t, nv), row),
        out_shape=jax.ShapeDtypeStruct((n, nv), BF16),
        scratch_shapes=[pltpu.VMEM((nv, V7X_LANES), F32),
                        pltpu.VMEM((t // CHUNK, nv, V7X_LANES), BF16),
                        pltpu.VMEM((t // CHUNK, nv, V7X_LANES), F32),
                        pltpu.VMEM((t, t), BF16), pltpu.VMEM((t, nv), F32)],
        compiler_params=_cparams(32 << 20),
        name="gla",
    )(q, k, v, r, la, ng)


def _attn_kernel(q_ref, kp_ref, kc_ref, vp_ref, vc_ref, bias_ref, *refs, n_cast):
    src_refs = refs[:n_cast]
    y_ref = refs[n_cast]
    dst_refs = refs[n_cast + 1:2 * n_cast + 1]
    s_ref, e_ref = refs[2 * n_cast + 1:]
    for src, dst in zip(src_refs, dst_refs):
        dst[...] = src[...].astype(dst.dtype)
    t = pl.program_id(1)
    tq = ATT_TILE

    def body(first_step):
        head_of_lane = lax.broadcasted_iota(jnp.int32, (1, ATT_GROUP * ATT_HD), 1) // ATT_HD
        for sub in range(ATT_SUB):
            pieces = []
            for back in (2, 1, 0):
                r0 = (sub - back) * tq
                pieces.append((kp_ref, vp_ref, r0 + ATT_SUB * tq, True) if r0 < 0
                              else (kc_ref, vc_ref, r0, False))
            qrows = slice(sub * tq, (sub + 1) * tq)
            for grp in range(ATT_HEADS // ATT_GROUP):
                cols = slice(grp * ATT_GROUP * ATT_HD, (grp + 1) * ATT_GROUP * ATT_HD)
                qg = q_ref[qrows, cols]
                for hh in range(ATT_GROUP):
                    mine = head_of_lane == hh
                    qm = jnp.where(mine, qg, jnp.zeros_like(qg))
                    head = grp * ATT_GROUP + hh
                    for j, (kr, _, r0, from_prev) in enumerate(pieces):
                        kc = slice(j * tq, (j + 1) * tq)
                        if first_step and from_prev:
                            s_ref[head, :, kc] = jnp.full((tq, tq), NEG_INF, F32)
                        else:
                            s_ref[head, :, kc] = _dot_nt(qm, kr[r0:r0 + tq, cols]) + bias_ref[head, :, kc]
                    e = jnp.exp2(s_ref[head] - jnp.max(s_ref[head], axis=-1, keepdims=True))
                    r = 1.0 / jnp.sum(e, axis=-1, keepdims=True)
                    e_ref[head] = e.astype(BF16)
                    pv = sum(_dot(e_ref[head, :, j * tq:(j + 1) * tq], vr[r0:r0 + tq, cols])
                             for j, (_, vr, r0, _) in enumerate(pieces)) * r
                    hd = slice(hh * ATT_HD, (hh + 1) * ATT_HD)
                    y_ref[qrows, cols.start + hd.start:cols.start + hd.stop] = pv[:, hd].astype(y_ref.dtype)

    pl.when(t == 0)(functools.partial(body, True))
    pl.when(t > 0)(functools.partial(body, False))


def _attn(q, k, v, bias, bsz, to_cast):
    n, w = q.shape
    t = ATT_SUB * ATT_TILE
    nt = n // bsz // t
    steps = bsz * nt
    cur = lambda b, i: (b * nt + i, 0)
    prev = lambda b, i: (b * nt + jnp.maximum(i - 1, 0), 0)
    blk = lambda im: pl.BlockSpec((t, w), im)
    for m in to_cast:
        assert m.shape[0] % (steps * BF16_SUBLANES) == 0, m.shape
    slab = lambda m: pl.BlockSpec((m.shape[0] // steps, m.shape[1]), cur)
    outs = pl.pallas_call(
        functools.partial(_attn_kernel, n_cast=len(to_cast)),
        grid=(bsz, nt),
        in_specs=[blk(cur), blk(prev), blk(cur), blk(prev), blk(cur), _const_spec(bias.shape)]
                 + [slab(m) for m in to_cast],
        out_specs=[blk(cur)] + [slab(m) for m in to_cast],
        out_shape=[jax.ShapeDtypeStruct((n, w), BF16)]
                  + [jax.ShapeDtypeStruct(m.shape, BF16) for m in to_cast],
        scratch_shapes=[pltpu.VMEM((ATT_HEADS, ATT_TILE, ATT_WIN), F32),
                        pltpu.VMEM((ATT_HEADS, ATT_TILE, ATT_WIN), BF16)],
        compiler_params=_cparams(48 << 20),
        name="attn",
    )(q, k, k, v, v, bias, *to_cast)
    return outs[0], outs[1:]


def _att_bias_table(rel_bias):
    nh, ntab = rel_bias.shape
    period = ATT_TILE + ATT_WIN
    n_lo = (ATT_TILE - 1) - (CHUNK - 1)
    rb = rel_bias.astype(F32)
    vec = jnp.concatenate([jnp.broadcast_to(rb[:, :1], (nh, n_lo)), rb,
                           jnp.broadcast_to(rb[:, -1:], (nh, period - n_lo - ntab))], axis=1)
    rolled = jnp.roll(vec[:, ::-1], -ATT_TILE, axis=1)
    flat = jnp.tile(rolled, (1, ATT_TILE))[:, :ATT_TILE * (period - 1)]
    table = flat.reshape(nh, ATT_TILE, period - 1)[:, :, :ATT_WIN]
    qc = jnp.arange(ATT_TILE)[:, None] // CHUNK + 2 * ATT_TILE // CHUNK
    kc = jnp.arange(ATT_WIN)[None, :] // CHUNK
    in_band = (kc <= qc) & (kc >= qc - (ATT_BAND - 1))
    return jnp.where(in_band[None], table * LOG2E, NEG_INF)


def _conv_tile(x_ref, w_ref, b_ref, lng_ref, lnb_ref, ext_ref, co_ref, y_ref):
    tile, width = x_ref.shape
    nslab = width // V7X_LANES
    out0 = CONV_HALO + tile - 8 * CONV_SEG
    in0 = out0 - (CONV_K - 1)
    lanes = [slice(c * V7X_LANES, (c + 1) * V7X_LANES) for c in range(nslab)]

    @pl.when(pl.program_id(1) == 0)
    def _():
        ext_ref[:, 0:CONV_HALO, :] = jnp.zeros((nslab, CONV_HALO, V7X_LANES), F32)

    for c in range(nslab):
        ext_ref[c, CONV_HALO:, :] = x_ref[:, lanes[c]]

    for c in range(nslab):
        for i0 in range(0, CONV_SEG, CONV_IB):
            accs = [None] * CONV_IB
            for j in range(CONV_K):
                wj = jnp.broadcast_to(w_ref[j:j + 1, lanes[c]], (8, V7X_LANES))
                for a in range(CONV_IB):
                    term = wj * ext_ref[c, pl.ds(in0 + i0 + a + j, 8, stride=CONV_SEG), :]
                    accs[a] = term if accs[a] is None else accs[a] + term
            for a in range(CONV_IB):
                co_ref[c, pl.ds(out0 + i0 + a, 8, stride=CONV_SEG), :] = accs[a]

    for r0 in range(0, tile, CONV_NB):
        parts = [co_ref[c, CONV_HALO + r0:CONV_HALO + r0 + CONV_NB, :] + b_ref[:, lanes[c]]
                 for c in range(nslab)]
        mu = jnp.sum(sum(parts), axis=-1, keepdims=True) / width
        cen = [q - mu for q in parts]
        var = jnp.sum(sum(q * q for q in cen), axis=-1, keepdims=True) / width
        inv = lax.rsqrt(var + EPS)
        for c in range(nslab):
            y = cen[c] * inv * lng_ref[:, lanes[c]] + lnb_ref[:, lanes[c]]
            y_ref[r0:r0 + CONV_NB, lanes[c]] = _silu(y).astype(y_ref.dtype)

    for c in range(nslab):
        ext_ref[c, 0:CONV_HALO, :] = ext_ref[c, tile:tile + CONV_HALO, :]


def _merge_kernel(h_ref, ya_ref, yb_ref, yc_ref, yd_ref, g1_ref, wg_ref, bg_ref, wb_ref, wo_ref, o_ref):
    h = h_ref[...]
    xn = _rms(h, g1_ref[...]).astype(BF16)
    merged = None
    for n, y_ref in enumerate((ya_ref, yb_ref, yc_ref, yd_ref)):
        gate = _sigmoid(_dot(xn, wg_ref[n]) + bg_ref[n])
        term = gate * _dot(y_ref[...], wb_ref[n])
        merged = term if merged is None else merged + term
    o_ref[...] = h + _dot(merged.astype(BF16), wo_ref[...])


def _merge(h, ys, g1, wg, bg, wb, wo):
    n, d = h.shape
    t = DENSE_TILE
    row = lambda i: (i, 0)
    consts = (g1, wg, bg, wb, wo)
    return pl.pallas_call(
        _merge_kernel,
        grid=(n // t,),
        in_specs=[pl.BlockSpec((t, d), row)] + [pl.BlockSpec((t, y.shape[1]), row) for y in ys]
                 + [_const_spec(c.shape) for c in consts],
        out_specs=pl.BlockSpec((t, d), row),
        out_shape=jax.ShapeDtypeStruct((n, d), F32),
        compiler_params=_cparams(58 << 20),
        name="merge",
    )(h, *ys, *consts)


def _ffn_ple_kernel(h_ref, p_ref, g2_ref, w1_ref, w2_ref, g3_ref, wpg_ref, bpg_ref, wp_ref, gf_ref,
                    o_ref, *, final_norm, ff_chunk):
    h = h_ref[...]
    hn = _rms(h, g2_ref[...]).astype(BF16)
    acc = h
    for j in range(w1_ref.shape[1] // ff_chunk):
        cols = slice(j * ff_chunk, (j + 1) * ff_chunk)
        a = jnp.maximum(_dot(hn, w1_ref[:, cols]), 0.0)
        acc = acc + _dot((a * a).astype(BF16), w2_ref[cols, :])
    hg = _rms(acc, g3_ref[...]).astype(BF16)
    gate = _sigmoid(_dot(hg, wpg_ref[...]) + bpg_ref[...])
    out = acc + gate * _dot(p_ref[...].astype(BF16), wp_ref[...])
    if final_norm:
        out = _rms(out, gf_ref[...])
    o_ref[...] = out


def _ffn_ple(h, p, layer, g2, w1, w2, g3, wpg, bpg, wp, gf, final_norm):
    n, d = h.shape
    t = DENSE_TILE
    row = lambda i: (i, 0)
    consts = (g2, w1, w2, g3, wpg, bpg, wp, gf)
    return pl.pallas_call(
        functools.partial(_ffn_ple_kernel, final_norm=final_norm, ff_chunk=1024),
        grid=(n // t,),
        in_specs=[pl.BlockSpec((t, d), row),
                  pl.BlockSpec((None, t, p.shape[2]), lambda i: (layer, i, 0))]
                 + [_const_spec(c.shape) for c in consts],
        out_specs=pl.BlockSpec((t, d), row),
        out_shape=jax.ShapeDtypeStruct((n, d), F32),
        compiler_params=_cparams(58 << 20),
        name="ffn_ple",
    )(h, p, *consts)


def kernel(x, p, norm1_g, w_in, sg_ln_g, sg_ln_b, sg_w, sg_b, gla_w_a2, gla_b_a, gla_norm_g, att_rel_bias, conv_dw_w, conv_dw_b, conv_ln_g, conv_ln_b, w_branch, w_gate, b_gate, w_out, norm2_g, w_ff1, w_ff2, norm3_g, w_ple_gate, b_ple_gate, w_ple, final_g):
    bsz, seq, d = x.shape
    depth = w_in.shape[0]
    n = bsz * seq
    bw = d // 2
    nqk = GLA_HEADS * GLA_DK
    assert seq % TOK_TILE == 0 and seq % (ATT_SUB * ATT_TILE) == 0 and n % DENSE_TILE == 0
    assert bw == SG_GROUPS * V7X_LANES
    row2 = lambda a: a.reshape(1, -1)

    c_ga = 2 * bw + 2 * nqk + 2 * bw
    c_att = c_ga + GLA_RANK

    h = x.reshape(n, d)
    p3 = p.reshape(depth, n, -1)
    for i in range(depth):
        w = jnp.concatenate([w_in[i, :, :c_ga], w_in[i, :, c_att:],
                             jnp.pad(w_in[i, :, c_ga:c_att], ((0, 0), (0, V7X_LANES - GLA_RANK)))],
                            axis=1).astype(BF16)
        wa2 = jnp.pad(gla_w_a2[i].astype(BF16), ((0, V7X_LANES - GLA_RANK), (0, 0)))
        sgbias = jnp.repeat(sg_b[i].T, bw // SG_GROUPS, axis=1)
        g1 = row2(norm1_g[i])
        (y_a, gq, gk, gv, gr, la, aq, ak, av, y_d) = _in_proj(
            h, g1, w, row2(sg_ln_g[i]), row2(sg_ln_b[i]), sg_w[i], sgbias, wa2, row2(gla_b_a[i]),
            jnp.pad(conv_dw_w[i], ((0, 1), (0, 0))), row2(conv_dw_b[i]),
            row2(conv_ln_g[i]), row2(conv_ln_b[i]), bsz)
        y_b = _gla(gq, gk, gv, gr, la, row2(gla_norm_g[i]), bsz)
        y_c, (wg, wb, wo, w1, w2, wpg) = _attn(
            aq, ak, av, _att_bias_table(att_rel_bias[i]), bsz,
            (w_gate[i].reshape(-1, d), w_branch[i].reshape(-1, d), w_out[i], w_ff1[i], w_ff2[i],
             w_ple_gate[i]))
        h = _merge(h, (y_a, y_b, y_c, y_d), g1, wg.reshape(w_gate.shape[1:]), b_gate[i][:, None, :],
                   wb.reshape(w_branch.shape[1:]), wo)
        h = _ffn_ple(h, p3, i, row2(norm2_g[i]), w1, w2, row2(norm3_g[i]), wpg,
                     row2(b_ple_gate[i]), w_ple[i].astype(BF16), row2(final_g),
                     final_norm=(i == depth - 1))
    return h.reshape(bsz, seq, d)
```

```python
import functools

import jax
import jax.numpy as jnp
from jax import lax
from jax.experimental import pallas as pl
from jax.experimental.pallas import tpu as pltpu

F32 = jnp.float32
BF16 = jnp.bfloat16

EPS = 1e-6
NEG_INF = -1e30
LOG2E = 1.4426950408889634

CHUNK = 64
SG_BLOCK = 128
SG_GROUPS = 4
GLA_HEADS = 4
GLA_DK = 64
GLA_DV = 128
GLA_RANK = 16
GLA_TAU = 16.0
ATT_HEADS = 8
ATT_HD = 64
ATT_BAND = 9
MAX_REL = 256
CONV_K = 31

V7X_LANES = 128
BF16_SUBLANES = 16
V7X_VMEM_BYTES = 64 * 1024 * 1024

TOK_TILE = 512
DENSE_TILE = 1024
ATT_TILE = 256
ATT_WIN = 3 * ATT_TILE
ATT_GROUP = 4
ATT_SUB = 2
CONV_HALO = 64
CONV_SEG = 68
CONV_IB = 17
CONV_NB = 64


def _cparams(vmem_bytes, flags=None):
    return pltpu.CompilerParams(
        dimension_semantics=None,
        vmem_limit_bytes=int(min(vmem_bytes, V7X_VMEM_BYTES - (6 << 20))),
        flags=flags,
    )


def _const_spec(shape):
    nd = len(shape)
    return pl.BlockSpec(shape, lambda *_: (0,) * nd, pipeline_mode=pl.Buffered(1))


def _rms(x, g):
    ms = jnp.mean(x * x, axis=-1, keepdims=True)
    return x * lax.rsqrt(ms + EPS) * g


def _layer_norm(x, g, b):
    mu = jnp.mean(x, axis=-1, keepdims=True)
    xc = x - mu
    var = jnp.mean(xc * xc, axis=-1, keepdims=True)
    return xc * lax.rsqrt(var + EPS) * g + b


def _dot(a, b):
    return jnp.dot(a, b, preferred_element_type=F32)


def _dot_nt(a, b):
    return lax.dot_general(a, b, (((1,), (1,)), ((), ())), preferred_element_type=F32)


def _dot_tn(a, b):
    return lax.dot_general(a, b, (((0,), (0,)), ((), ())), preferred_element_type=F32)


def _sigmoid(x):
    return 0.5 * jnp.tanh(0.5 * x) + 0.5


def _silu(x):
    return x * _sigmoid(x)


def _log_sigmoid(x):
    return jnp.minimum(x, 0.0) - jnp.log(1.0 + jnp.exp(-jnp.abs(x)))


def _inproj_kernel(h_ref, g1_ref, w_ref,
                   lng_ref, lnb_ref, sgw_ref, sgbias_ref, wa2_ref, ba_ref,
                   cw_ref, cb_ref, clg_ref, clb_ref,
                   ya_ref, gq_ref, gk_ref, gv_ref, gr_ref, la_ref,
                   aq_ref, ak_ref, av_ref, yd_ref, cy_ref, ext_ref, co_ref):
    tile = h_ref.shape[0]
    xn = _rms(h_ref[...], g1_ref[...]).astype(BF16)
    bw = ya_ref.shape[1]
    o_gla = 2 * bw
    o_att = o_gla + 2 * GLA_HEADS * GLA_DK + 2 * bw
    o_cv = o_att + 3 * bw
    o_ga = o_cv + 2 * bw

    pc = _dot(xn, w_ref[:, o_cv:o_ga])
    cw = pc.shape[1] // 2
    cy_ref[...] = pc[:, :cw] * _sigmoid(pc[:, cw:])
    _conv_tile(cy_ref, cw_ref, cb_ref, clg_ref, clb_ref, ext_ref, co_ref, yd_ref)

    ps = _dot(xn, w_ref[:, 0:o_gla])
    half = ps.shape[1] // 2
    u = jax.nn.gelu(ps[:, :half])
    vn = _layer_norm(jax.nn.gelu(ps[:, half:]), lng_ref[...], lnb_ref[...]).astype(BF16)
    pi = lax.broadcasted_iota(jnp.int32, (SG_BLOCK, SG_BLOCK), 0) // CHUNK
    pj = lax.broadcasted_iota(jnp.int32, (SG_BLOCK, SG_BLOCK), 1) // CHUNK
    cg = half // SG_GROUPS
    nblk = tile // SG_BLOCK
    for g in range(SG_GROUPS):
        wm = jnp.where(pj <= pi, sgw_ref[g], 0.0).astype(BF16)
        cols = slice(g * cg, (g + 1) * cg)
        vblocks = jnp.concatenate([vn[n * SG_BLOCK:(n + 1) * SG_BLOCK, cols] for n in range(nblk)], axis=1)
        mixed = _dot(wm, vblocks)
        for n in range(nblk):
            rows = slice(n * SG_BLOCK, (n + 1) * SG_BLOCK)
            ya_ref[rows, cols] = (u[rows, cols] * (mixed[:, n * cg:(n + 1) * cg] + sgbias_ref[:, cols])
                                  ).astype(ya_ref.dtype)

    pg = _dot(xn, w_ref[:, o_gla:o_att])
    nqk = GLA_HEADS * GLA_DK
    nv = GLA_HEADS * GLA_DV
    gq_ref[...] = (pg[:, :nqk] * (GLA_DK ** -0.5)).astype(gq_ref.dtype)
    gk_ref[...] = pg[:, nqk:2 * nqk]
    gv_ref[...] = pg[:, 2 * nqk:2 * nqk + nv].astype(gv_ref.dtype)
    gr_ref[...] = pg[:, 2 * nqk + nv:]
    a_lr = _dot(xn, w_ref[:, o_ga:]).astype(BF16)
    la_ref[...] = _log_sigmoid(_dot(a_lr, wa2_ref[...]) + ba_ref[...]) / GLA_TAU

    pa = _dot(xn, w_ref[:, o_att:o_cv])
    na = ATT_HEADS * ATT_HD
    aq_ref[...] = (pa[:, :na] * (ATT_HD ** -0.5 * LOG2E)).astype(aq_ref.dtype)
    ak_ref[...] = pa[:, na:2 * na].astype(ak_ref.dtype)
    av_ref[...] = pa[:, 2 * na:].astype(av_ref.dtype)


def _in_proj(h, g1, w, lng, lnb, sgw, sgbias, wa2, ba, cw, cb, clg, clb, bsz):
    n, d = h.shape
    t = TOK_TILE
    nt = n // bsz // t
    bw = d // 2
    nqk = GLA_HEADS * GLA_DK
    assert CONV_HALO + t >= 8 * CONV_SEG + CONV_K - 1 and 8 * CONV_SEG >= t
    assert CONV_SEG % CONV_IB == 0 and CONV_SEG % 8 == 4
    row = lambda b, i: (b * nt + i, 0)
    slabs = (bw // V7X_LANES, t + CONV_HALO, V7X_LANES)
    outs = [
        ((n, bw), BF16),
        ((n, nqk), BF16),
        ((n, nqk), F32),
        ((n, bw), BF16),
        ((n, bw), F32),
        ((n, nqk), F32),
        ((n, bw), BF16),
        ((n, bw), BF16),
        ((n, bw), BF16),
        ((n, bw), BF16),
    ]
    consts = (g1, w, lng, lnb, sgw, sgbias, wa2, ba, cw, cb, clg, clb)
    return pl.pallas_call(
        _inproj_kernel,
        grid=(bsz, nt),
        in_specs=[pl.BlockSpec((t, d), row)] + [_const_spec(c.shape) for c in consts],
        out_specs=[pl.BlockSpec((t, s[1]), row) for s, _ in outs],
        out_shape=[jax.ShapeDtypeStruct(s, dt) for s, dt in outs],
        scratch_shapes=[pltpu.VMEM((t, bw), F32), pltpu.VMEM(slabs, F32), pltpu.VMEM(slabs, F32)],
        compiler_params=_cparams(56 << 20),
        name="in_proj",
    )(h, *consts)


def _gla_kernel(q_ref, k_ref, v_ref, r_ref, la_ref, ng_ref, y_ref,
                st_ref, stb_ref, upd_ref, later_ref, o_ref):
    tile = q_ref.shape[0]
    nchunk = tile // CHUNK
    pair_rows = 2 * GLA_DV
    npair = GLA_HEADS // 2

    @pl.when(pl.program_id(1) == 0)
    def _():
        st_ref[...] = jnp.zeros_like(st_ref)
        ri = lax.broadcasted_iota(jnp.int32, (tile, tile), 0)
        ci = lax.broadcasted_iota(jnp.int32, (tile, tile), 1)
        later_ref[...] = jnp.where((ci > ri) & (ci // CHUNK == ri // CHUNK), 1.0, 0.0).astype(BF16)

    la = la_ref[...]
    hi = la.astype(BF16)
    lo = (la - hi.astype(F32)).astype(BF16)
    sfx = _dot(later_ref[...], hi) + _dot(later_ref[...], lo)
    kd = (k_ref[...] * jnp.exp(sfx)).astype(BF16)
    total = la + sfx

    own_lanes = (lax.broadcasted_iota(jnp.int32, (pair_rows, V7X_LANES), 1) // GLA_DK
                 == lax.broadcasted_iota(jnp.int32, (pair_rows, V7X_LANES), 0) // GLA_DV)

    for c in range(nchunk):
        rows = slice(c * CHUNK, (c + 1) * CHUNK)
        for p in range(npair):
            upd = _dot_tn(v_ref[rows, p * pair_rows:(p + 1) * pair_rows],
                          kd[rows, p * V7X_LANES:(p + 1) * V7X_LANES])
            upd_ref[c, p * pair_rows:(p + 1) * pair_rows, :] = jnp.where(own_lanes, upd, 0.0)

    st = [st_ref[p * pair_rows:(p + 1) * pair_rows, :] for p in range(npair)]
    for c in range(nchunk):
        decay = jnp.exp(total[c * CHUNK:c * CHUNK + 1, :])
        for p in range(npair):
            prow = slice(p * pair_rows, (p + 1) * pair_rows)
            st[p] = st[p] * decay[:, p * V7X_LANES:(p + 1) * V7X_LANES] + upd_ref[c, prow, :]
            stb_ref[c, prow, :] = st[p].astype(BF16)
    for p in range(npair):
        st_ref[p * pair_rows:(p + 1) * pair_rows, :] = st[p]

    for c in range(nchunk):
        rows = slice(c * CHUNK, (c + 1) * CHUNK)
        for p in range(npair):
            prow = slice(p * pair_rows, (p + 1) * pair_rows)
            o_ref[rows, prow] = _dot_nt(q_ref[rows, p * V7X_LANES:(p + 1) * V7X_LANES], stb_ref[c, prow, :])

    for h in range(GLA_HEADS):
        cols = slice(h * GLA_DV, (h + 1) * GLA_DV)
        o = o_ref[:, cols]
        o = o * lax.rsqrt(jnp.mean(o * o, axis=-1, keepdims=True) + EPS) * ng_ref[:, cols]
        y_ref[:, cols] = (o * _silu(r_ref[:, cols])).astype(y_ref.dtype)


def _gla(q, k, v, r, la, ng, bsz):
    n = q.shape[0]
    t = TOK_TILE
    nt = n // bsz // t
    nqk = GLA_HEADS * GLA_DK
    nv = GLA_HEADS * GLA_DV
    row = lambda b, i: (b * nt + i, 0)
    return pl.pallas_call(
        _gla_kernel,
        grid=(bsz, nt),
        in_specs=[pl.BlockSpec((t, nqk), row), pl.BlockSpec((t, nqk), row),
                  pl.BlockSpec((t, nv), row), pl.BlockSpec((t, nv), row),
                  pl.BlockSpec((t, nqk), row), _const_spec(ng.shape)],
        out_specs=pl.BlockSpec((t, nv), row),
        out_shape=jax.ShapeDtypeStruct((n, nv), BF16),
        scratch_shapes=[pltpu.VMEM((nv, V7X_LANES), F32),
                        pltpu.VMEM((t // CHUNK, nv, V7X_LANES), BF16),
                        pltpu.VMEM((t // CHUNK, nv, V7X_LANES), F32),
                        pltpu.VMEM((t, t), BF16), pltpu.VMEM((t, nv), F32)],
        compiler_params=_cparams(32 << 20),
        name="gla",
    )(q, k, v, r, la, ng)


def _attn_kernel(q_ref, kp_ref, kc_ref, vp_ref, vc_ref, bias_ref, *refs, n_cast):
    src_refs = refs[:n_cast]
    y_ref = refs[n_cast]
    dst_refs = refs[n_cast + 1:2 * n_cast + 1]
    s_ref, e_ref = refs[2 * n_cast + 1:]
    for src, dst in zip(src_refs, dst_refs):
        dst[...] = src[...].astype(dst.dtype)
    t = pl.program_id(1)
    tq = ATT_TILE

    def body(first_step):
        head_of_lane = lax.broadcasted_iota(jnp.int32, (1, ATT_GROUP * ATT_HD), 1) // ATT_HD
        for sub in range(ATT_SUB):
            pieces = []
            for back in (2, 1, 0):
                r0 = (sub - back) * tq
                pieces.append((kp_ref, vp_ref, r0 + ATT_SUB * tq, True) if r0 < 0
                              else (kc_ref, vc_ref, r0, False))
            qrows = slice(sub * tq, (sub + 1) * tq)
            for grp in range(ATT_HEADS // ATT_GROUP):
                cols = slice(grp * ATT_GROUP * ATT_HD, (grp + 1) * ATT_GROUP * ATT_HD)
                qg = q_ref[qrows, cols]
                for hh in range(ATT_GROUP):
                    mine = head_of_lane == hh
                    qm = jnp.where(mine, qg, jnp.zeros_like(qg))
                    head = grp * ATT_GROUP + hh
                    for j, (kr, _, r0, from_prev) in enumerate(pieces):
                        kc = slice(j * tq, (j + 1) * tq)
                        if first_step and from_prev:
                            s_ref[head, :, kc] = jnp.full((tq, tq), NEG_INF, F32)
                        else:
                            s_ref[head, :, kc] = _dot_nt(qm, kr[r0:r0 + tq, cols]) + bias_ref[head, :, kc]
                    e = jnp.exp2(s_ref[head] - jnp.max(s_ref[head], axis=-1, keepdims=True))
                    r = 1.0 / jnp.sum(e, axis=-1, keepdims=True)
                    e_ref[head] = e.astype(BF16)
                    pv = sum(_dot(e_ref[head, :, j * tq:(j + 1) * tq], vr[r0:r0 + tq, cols])
                             for j, (_, vr, r0, _) in enumerate(pieces)) * r
                    hd = slice(hh * ATT_HD, (hh + 1) * ATT_HD)
                    y_ref[qrows, cols.start + hd.start:cols.start + hd.stop] = pv[:, hd].astype(y_ref.dtype)

    pl.when(t == 0)(functools.partial(body, True))
    pl.when(t > 0)(functools.partial(body, False))


def _attn(q, k, v, bias, bsz, to_cast, layer):
    n, w = q.shape
    t = ATT_SUB * ATT_TILE
    nt = n // bsz // t
    steps = bsz * nt
    cur = lambda b, i: (b * nt + i, 0)
    prev = lambda b, i: (b * nt + jnp.maximum(i - 1, 0), 0)
    blk = lambda im: pl.BlockSpec((t, w), im)
    for m in to_cast:
        assert m.shape[1] % (steps * BF16_SUBLANES) == 0, m.shape
    src_slab = lambda m: pl.BlockSpec((None, m.shape[1] // steps, m.shape[2]),
                                      lambda b, i: (layer, b * nt + i, 0))
    dst_slab = lambda m: pl.BlockSpec((m.shape[1] // steps, m.shape[2]), cur)
    outs = pl.pallas_call(
        functools.partial(_attn_kernel, n_cast=len(to_cast)),
        grid=(bsz, nt),
        in_specs=[blk(cur), blk(prev), blk(cur), blk(prev), blk(cur), _const_spec(bias.shape)]
                 + [src_slab(m) for m in to_cast],
        out_specs=[blk(cur)] + [dst_slab(m) for m in to_cast],
        out_shape=[jax.ShapeDtypeStruct((n, w), BF16)]
                  + [jax.ShapeDtypeStruct(m.shape[1:], BF16) for m in to_cast],
        scratch_shapes=[pltpu.VMEM((ATT_HEADS, ATT_TILE, ATT_WIN), F32),
                        pltpu.VMEM((ATT_HEADS, ATT_TILE, ATT_WIN), BF16)],
        compiler_params=_cparams(48 << 20),
        name="attn",
    )(q, k, k, v, v, bias, *to_cast)
    return outs[0], outs[1:]


def _att_bias_table(rel_bias):
    nh, ntab = rel_bias.shape
    period = ATT_TILE + ATT_WIN
    n_lo = (ATT_TILE - 1) - (CHUNK - 1)
    rb = rel_bias.astype(F32)
    vec = jnp.concatenate([jnp.broadcast_to(rb[:, :1], (nh, n_lo)), rb,
                           jnp.broadcast_to(rb[:, -1:], (nh, period - n_lo - ntab))], axis=1)
    rolled = jnp.roll(vec[:, ::-1], -ATT_TILE, axis=1)
    flat = jnp.tile(rolled, (1, ATT_TILE))[:, :ATT_TILE * (period - 1)]
    table = flat.reshape(nh, ATT_TILE, period - 1)[:, :, :ATT_WIN]
    qc = jnp.arange(ATT_TILE)[:, None] // CHUNK + 2 * ATT_TILE // CHUNK
    kc = jnp.arange(ATT_WIN)[None, :] // CHUNK
    in_band = (kc <= qc) & (kc >= qc - (ATT_BAND - 1))
    return jnp.where(in_band[None], table * LOG2E, NEG_INF)


def _conv_tile(x_ref, w_ref, b_ref, lng_ref, lnb_ref, ext_ref, co_ref, y_ref):
    tile, width = x_ref.shape
    nslab = width // V7X_LANES
    out0 = CONV_HALO + tile - 8 * CONV_SEG
    in0 = out0 - (CONV_K - 1)
    lanes = [slice(c * V7X_LANES, (c + 1) * V7X_LANES) for c in range(nslab)]

    @pl.when(pl.program_id(1) == 0)
    def _():
        ext_ref[:, 0:CONV_HALO, :] = jnp.zeros((nslab, CONV_HALO, V7X_LANES), F32)

    for c in range(nslab):
        ext_ref[c, CONV_HALO:, :] = x_ref[:, lanes[c]]

    for c in range(nslab):
        for i0 in range(0, CONV_SEG, CONV_IB):
            accs = [None] * CONV_IB
            for j in range(CONV_K):
                wj = jnp.broadcast_to(w_ref[j:j + 1, lanes[c]], (8, V7X_LANES))
                for a in range(CONV_IB):
                    term = wj * ext_ref[c, pl.ds(in0 + i0 + a + j, 8, stride=CONV_SEG), :]
                    accs[a] = term if accs[a] is None else accs[a] + term
            for a in range(CONV_IB):
                co_ref[c, pl.ds(out0 + i0 + a, 8, stride=CONV_SEG), :] = accs[a]

    for r0 in range(0, tile, CONV_NB):
        parts = [co_ref[c, CONV_HALO + r0:CONV_HALO + r0 + CONV_NB, :] + b_ref[:, lanes[c]]
                 for c in range(nslab)]
        mu = jnp.sum(sum(parts), axis=-1, keepdims=True) / width
        cen = [q - mu for q in parts]
        var = jnp.sum(sum(q * q for q in cen), axis=-1, keepdims=True) / width
        inv = lax.rsqrt(var + EPS)
        for c in range(nslab):
            y = cen[c] * inv * lng_ref[:, lanes[c]] + lnb_ref[:, lanes[c]]
            y_ref[r0:r0 + CONV_NB, lanes[c]] = _silu(y).astype(y_ref.dtype)

    for c in range(nslab):
        ext_ref[c, 0:CONV_HALO, :] = ext_ref[c, tile:tile + CONV_HALO, :]


def _merge_kernel(h_ref, ya_ref, yb_ref, yc_ref, yd_ref, g1_ref, wg_ref, bg_ref, wb_ref, wo_ref, o_ref):
    h = h_ref[...]
    xn = _rms(h, g1_ref[...]).astype(BF16)
    merged = None
    for n, y_ref in enumerate((ya_ref, yb_ref, yc_ref, yd_ref)):
        gate = _sigmoid(_dot(xn, wg_ref[n]) + bg_ref[n])
        term = gate * _dot(y_ref[...], wb_ref[n])
        merged = term if merged is None else merged + term
    o_ref[...] = h + _dot(merged.astype(BF16), wo_ref[...])


def _merge(h, ys, g1, wg, bg, wb, wo):
    n, d = h.shape
    t = DENSE_TILE
    row = lambda i: (i, 0)
    consts = (g1, wg, bg, wb, wo)
    return pl.pallas_call(
        _merge_kernel,
        grid=(n // t,),
        in_specs=[pl.BlockSpec((t, d), row)] + [pl.BlockSpec((t, y.shape[1]), row) for y in ys]
                 + [_const_spec(c.shape) for c in consts],
        out_specs=pl.BlockSpec((t, d), row),
        out_shape=jax.ShapeDtypeStruct((n, d), F32),
        compiler_params=_cparams(58 << 20),
        name="merge",
    )(h, *ys, *consts)


def _ffn_ple_kernel(h_ref, p_ref, g2_ref, w1_ref, w2_ref, g3_ref, wpg_ref, bpg_ref, wp_ref, gf_ref,
                    o_ref, *, final_norm, ff_chunk):
    h = h_ref[...]
    hn = _rms(h, g2_ref[...]).astype(BF16)
    acc = h
    for j in range(w1_ref.shape[1] // ff_chunk):
        cols = slice(j * ff_chunk, (j + 1) * ff_chunk)
        a = jnp.maximum(_dot(hn, w1_ref[:, cols]), 0.0)
        acc = acc + _dot((a * a).astype(BF16), w2_ref[cols, :])
    hg = _rms(acc, g3_ref[...]).astype(BF16)
    gate = _sigmoid(_dot(hg, wpg_ref[...]) + bpg_ref[...])
    out = acc + gate * _dot(p_ref[...].astype(BF16), wp_ref[...])
    if final_norm:
        out = _rms(out, gf_ref[...])
    o_ref[...] = out


def _ffn_ple(h, p, layer, g2, w1, w2, g3, wpg, bpg, wp, gf, final_norm):
    n, d = h.shape
    t = DENSE_TILE
    row = lambda i: (i, 0)
    consts = (g2, w1, w2, g3, wpg, bpg, wp, gf)
    return pl.pallas_call(
        functools.partial(_ffn_ple_kernel, final_norm=final_norm, ff_chunk=1024),
        grid=(n // t,),
        in_specs=[pl.BlockSpec((t, d), row),
                  pl.BlockSpec((None, t, p.shape[2]), lambda i: (layer, i, 0))]
                 + [_const_spec(c.shape) for c in consts],
        out_specs=pl.BlockSpec((t, d), row),
        out_shape=jax.ShapeDtypeStruct((n, d), F32),
        compiler_params=_cparams(58 << 20),
        name="ffn_ple",
    )(h, p, *consts)


def kernel(x, p, norm1_g, w_in, sg_ln_g, sg_ln_b, sg_w, sg_b, gla_w_a2, gla_b_a, gla_norm_g, att_rel_bias, conv_dw_w, conv_dw_b, conv_ln_g, conv_ln_b, w_branch, w_gate, b_gate, w_out, norm2_g, w_ff1, w_ff2, norm3_g, w_ple_gate, b_ple_gate, w_ple, final_g):
    bsz, seq, d = x.shape
    depth = w_in.shape[0]
    n = bsz * seq
    bw = d // 2
    nqk = GLA_HEADS * GLA_DK
    assert seq % TOK_TILE == 0 and seq % (ATT_SUB * ATT_TILE) == 0 and n % DENSE_TILE == 0
    assert bw == SG_GROUPS * V7X_LANES
    row2 = lambda a: a.reshape(1, -1)

    c_ga = 2 * bw + 2 * nqk + 2 * bw
    c_att = c_ga + GLA_RANK

    h = x.reshape(n, d)
    p3 = p.reshape(depth, n, -1)
    for i in range(depth):
        w = jnp.concatenate([w_in[i, :, :c_ga], w_in[i, :, c_att:],
                             jnp.pad(w_in[i, :, c_ga:c_att], ((0, 0), (0, V7X_LANES - GLA_RANK)))],
                            axis=1).astype(BF16)
        wa2 = jnp.pad(gla_w_a2[i].astype(BF16), ((0, V7X_LANES - GLA_RANK), (0, 0)))
        sgbias = jnp.repeat(sg_b[i].T, bw // SG_GROUPS, axis=1)
        g1 = row2(norm1_g[i])
        (y_a, gq, gk, gv, gr, la, aq, ak, av, y_d) = _in_proj(
            h, g1, w, row2(sg_ln_g[i]), row2(sg_ln_b[i]), sg_w[i], sgbias, wa2, row2(gla_b_a[i]),
            jnp.pad(conv_dw_w[i], ((0, 1), (0, 0))), row2(conv_dw_b[i]),
            row2(conv_ln_g[i]), row2(conv_ln_b[i]), bsz)
        y_b = _gla(gq, gk, gv, gr, la, row2(gla_norm_g[i]), bsz)
        y_c, (wg, wb, wo, w1, w2, wpg) = _attn(
            aq, ak, av, _att_bias_table(att_rel_bias[i]), bsz,
            (w_gate.reshape(depth, -1, d), w_branch.reshape(depth, -1, d), w_out, w_ff1, w_ff2,
             w_ple_gate), i)
        h = _merge(h, (y_a, y_b, y_c, y_d), g1, wg.reshape(w_gate.shape[1:]), b_gate[i][:, None, :],
                   wb.reshape(w_branch.shape[1:]), wo)
        h = _ffn_ple(h, p3, i, row2(norm2_g[i]), w1, w2, row2(norm3_g[i]), wpg,
                     row2(b_ple_gate[i]), w_ple[i].astype(BF16), row2(final_g),
                     final_norm=(i == depth - 1))
    return h.reshape(bsz, seq, d)
```

```python
import functools

import jax
import jax.numpy as jnp
from jax import lax
from jax.experimental import pallas as pl
from jax.experimental.pallas import tpu as pltpu

F32 = jnp.float32
BF16 = jnp.bfloat16

EPS = 1e-6
NEG_INF = -1e30
LOG2E = 1.4426950408889634

CHUNK = 64
SG_BLOCK = 128
SG_GROUPS = 4
GLA_HEADS = 4
GLA_DK = 64
GLA_DV = 128
GLA_RANK = 16
GLA_TAU = 16.0
ATT_HEADS = 8
ATT_HD = 64
ATT_BAND = 9
MAX_REL = 256
CONV_K = 31

V7X_LANES = 128
BF16_SUBLANES = 16
V7X_VMEM_BYTES = 64 * 1024 * 1024

TOK_TILE = 512
DENSE_TILE = 1024
ATT_TILE = 256
ATT_WIN = 3 * ATT_TILE
ATT_GROUP = 4
ATT_SUB = 4
CONV_HALO = 64
CONV_SEG = 68
CONV_IB = 17
CONV_NB = 64


def _cparams(vmem_bytes):
    return pltpu.CompilerParams(
        dimension_semantics=None,
        vmem_limit_bytes=int(min(vmem_bytes, V7X_VMEM_BYTES - (6 << 20))),
    )


def _const_spec(shape):
    nd = len(shape)
    return pl.BlockSpec(shape, lambda *_: (0,) * nd, pipeline_mode=pl.Buffered(1))


def _rms(x, g):
    ms = jnp.mean(x * x, axis=-1, keepdims=True)
    return x * lax.rsqrt(ms + EPS) * g


def _layer_norm(x, g, b):
    mu = jnp.mean(x, axis=-1, keepdims=True)
    xc = x - mu
    var = jnp.mean(xc * xc, axis=-1, keepdims=True)
    return xc * lax.rsqrt(var + EPS) * g + b


def _dot(a, b):
    return jnp.dot(a, b, preferred_element_type=F32)


def _dot_nt(a, b):
    return lax.dot_general(a, b, (((1,), (1,)), ((), ())), preferred_element_type=F32)


def _dot_tn(a, b):
    return lax.dot_general(a, b, (((0,), (0,)), ((), ())), preferred_element_type=F32)


def _sigmoid(x):
    return 0.5 * jnp.tanh(0.5 * x) + 0.5


def _silu(x):
    return x * _sigmoid(x)


def _log_sigmoid(x):
    return jnp.minimum(x, 0.0) - jnp.log(1.0 + jnp.exp(-jnp.abs(x)))


def _inproj_kernel(h_ref, g1_ref, w_ref,
                   lng_ref, lnb_ref, sgw_ref, sgbias_ref, wa2_ref, ba_ref,
                   cw_ref, cb_ref, clg_ref, clb_ref,
                   ya_ref, gq_ref, gk_ref, gv_ref, gr_ref, la_ref,
                   aq_ref, ak_ref, av_ref, yd_ref, cy_ref, ext_ref, co_ref):
    tile = h_ref.shape[0]
    xn = _rms(h_ref[...], g1_ref[...]).astype(BF16)
    bw = ya_ref.shape[1]
    o_gla = 2 * bw
    o_att = o_gla + 2 * GLA_HEADS * GLA_DK + 2 * bw
    o_cv = o_att + 3 * bw
    o_ga = o_cv + 2 * bw

    pc = _dot(xn, w_ref[:, o_cv:o_ga])
    cw = pc.shape[1] // 2
    cy_ref[...] = pc[:, :cw] * _sigmoid(pc[:, cw:])
    _conv_tile(cy_ref, cw_ref, cb_ref, clg_ref, clb_ref, ext_ref, co_ref, yd_ref)

    ps = _dot(xn, w_ref[:, 0:o_gla])
    half = ps.shape[1] // 2
    u = jax.nn.gelu(ps[:, :half])
    vn = _layer_norm(jax.nn.gelu(ps[:, half:]), lng_ref[...], lnb_ref[...]).astype(BF16)
    pi = lax.broadcasted_iota(jnp.int32, (SG_BLOCK, SG_BLOCK), 0) // CHUNK
    pj = lax.broadcasted_iota(jnp.int32, (SG_BLOCK, SG_BLOCK), 1) // CHUNK
    cg = half // SG_GROUPS
    nblk = tile // SG_BLOCK
    for g in range(SG_GROUPS):
        wm = jnp.where(pj <= pi, sgw_ref[g], 0.0).astype(BF16)
        cols = slice(g * cg, (g + 1) * cg)
        vblocks = jnp.concatenate([vn[n * SG_BLOCK:(n + 1) * SG_BLOCK, cols] for n in range(nblk)], axis=1)
        mixed = _dot(wm, vblocks)
        for n in range(nblk):
            rows = slice(n * SG_BLOCK, (n + 1) * SG_BLOCK)
            ya_ref[rows, cols] = (u[rows, cols] * (mixed[:, n * cg:(n + 1) * cg] + sgbias_ref[:, cols])
                                  ).astype(ya_ref.dtype)

    pg = _dot(xn, w_ref[:, o_gla:o_att])
    nqk = GLA_HEADS * GLA_DK
    nv = GLA_HEADS * GLA_DV
    gq_ref[...] = (pg[:, :nqk] * (GLA_DK ** -0.5)).astype(gq_ref.dtype)
    gk_ref[...] = pg[:, nqk:2 * nqk]
    gv_ref[...] = pg[:, 2 * nqk:2 * nqk + nv].astype(gv_ref.dtype)
    gr_ref[...] = pg[:, 2 * nqk + nv:]
    a_lr = _dot(xn, w_ref[:, o_ga:]).astype(BF16)
    la_ref[...] = _log_sigmoid(_dot(a_lr, wa2_ref[...]) + ba_ref[...]) / GLA_TAU

    pa = _dot(xn, w_ref[:, o_att:o_cv])
    na = ATT_HEADS * ATT_HD
    aq_ref[...] = (pa[:, :na] * (ATT_HD ** -0.5 * LOG2E)).astype(aq_ref.dtype)
    ak_ref[...] = pa[:, na:2 * na].astype(ak_ref.dtype)
    av_ref[...] = pa[:, 2 * na:].astype(av_ref.dtype)


def _in_proj(h, g1, w, lng, lnb, sgw, sgbias, wa2, ba, cw, cb, clg, clb, bsz):
    n, d = h.shape
    t = TOK_TILE
    nt = n // bsz // t
    bw = d // 2
    nqk = GLA_HEADS * GLA_DK
    assert CONV_HALO + t >= 8 * CONV_SEG + CONV_K - 1 and 8 * CONV_SEG >= t
    assert CONV_SEG % CONV_IB == 0 and CONV_SEG % 8 == 4
    row = lambda b, i: (b * nt + i, 0)
    slabs = (bw // V7X_LANES, t + CONV_HALO, V7X_LANES)
    outs = [
        ((n, bw), BF16),
        ((n, nqk), BF16),
        ((n, nqk), F32),
        ((n, bw), BF16),
        ((n, bw), F32),
        ((n, nqk), F32),
        ((n, bw), BF16),
        ((n, bw), BF16),
        ((n, bw), BF16),
        ((n, bw), BF16),
    ]
    consts = (g1, w, lng, lnb, sgw, sgbias, wa2, ba, cw, cb, clg, clb)
    return pl.pallas_call(
        _inproj_kernel,
        grid=(bsz, nt),
        in_specs=[pl.BlockSpec((t, d), row)] + [_const_spec(c.shape) for c in consts],
        out_specs=[pl.BlockSpec((t, s[1]), row) for s, _ in outs],
        out_shape=[jax.ShapeDtypeStruct(s, dt) for s, dt in outs],
        scratch_shapes=[pltpu.VMEM((t, bw), F32), pltpu.VMEM(slabs, F32), pltpu.VMEM(slabs, F32)],
        compiler_params=_cparams(56 << 20),
        name="in_proj",
    )(h, *consts)


def _gla_kernel(q_ref, k_ref, v_ref, r_ref, la_ref, ng_ref, y_ref,
                st_ref, stb_ref, upd_ref, later_ref, o_ref):
    tile = q_ref.shape[0]
    nchunk = tile // CHUNK
    pair_rows = 2 * GLA_DV
    npair = GLA_HEADS // 2

    @pl.when(pl.program_id(1) == 0)
    def _():
        st_ref[...] = jnp.zeros_like(st_ref)
        ri = lax.broadcasted_iota(jnp.int32, (tile, tile), 0)
        ci = lax.broadcasted_iota(jnp.int32, (tile, tile), 1)
        later_ref[...] = jnp.where((ci > ri) & (ci // CHUNK == ri // CHUNK), 1.0, 0.0).astype(BF16)

    la = la_ref[...]
    hi = la.astype(BF16)
    lo = (la - hi.astype(F32)).astype(BF16)
    sfx = _dot(later_ref[...], hi) + _dot(later_ref[...], lo)
    kd = (k_ref[...] * jnp.exp(sfx)).astype(BF16)
    total = la + sfx

    own_lanes = (lax.broadcasted_iota(jnp.int32, (pair_rows, V7X_LANES), 1) // GLA_DK
                 == lax.broadcasted_iota(jnp.int32, (pair_rows, V7X_LANES), 0) // GLA_DV)

    for c in range(nchunk):
        rows = slice(c * CHUNK, (c + 1) * CHUNK)
        for p in range(npair):
            upd = _dot_tn(v_ref[rows, p * pair_rows:(p + 1) * pair_rows],
                          kd[rows, p * V7X_LANES:(p + 1) * V7X_LANES])
            upd_ref[c, p * pair_rows:(p + 1) * pair_rows, :] = jnp.where(own_lanes, upd, 0.0)

    st = [st_ref[p * pair_rows:(p + 1) * pair_rows, :] for p in range(npair)]
    for c in range(nchunk):
        decay = jnp.exp(total[c * CHUNK:c * CHUNK + 1, :])
        for p in range(npair):
            prow = slice(p * pair_rows, (p + 1) * pair_rows)
            st[p] = st[p] * decay[:, p * V7X_LANES:(p + 1) * V7X_LANES] + upd_ref[c, prow, :]
            stb_ref[c, prow, :] = st[p].astype(BF16)
    for p in range(npair):
        st_ref[p * pair_rows:(p + 1) * pair_rows, :] = st[p]

    for c in range(nchunk):
        rows = slice(c * CHUNK, (c + 1) * CHUNK)
        for p in range(npair):
            prow = slice(p * pair_rows, (p + 1) * pair_rows)
            o_ref[rows, prow] = _dot_nt(q_ref[rows, p * V7X_LANES:(p + 1) * V7X_LANES], stb_ref[c, prow, :])

    for h in range(GLA_HEADS):
        cols = slice(h * GLA_DV, (h + 1) * GLA_DV)
        o = o_ref[:, cols]
        o = o * lax.rsqrt(jnp.mean(o * o, axis=-1, keepdims=True) + EPS) * ng_ref[:, cols]
        y_ref[:, cols] = (o * _silu(r_ref[:, cols])).astype(y_ref.dtype)


def _gla(q, k, v, r, la, ng, bsz):
    n = q.shape[0]
    t = TOK_TILE
    nt = n // bsz // t
    nqk = GLA_HEADS * GLA_DK
    nv = GLA_HEADS * GLA_DV
    row = lambda b, i: (b * nt + i, 0)
    return pl.pallas_call(
        _gla_kernel,
        grid=(bsz, nt),
        in_specs=[pl.BlockSpec((t, nqk), row), pl.BlockSpec((t, nqk), row),
                  pl.BlockSpec((t, nv), row), pl.BlockSpec((t, nv), row),
                  pl.BlockSpec((t, nqk), row), _const_spec(ng.shape)],
        out_specs=pl.BlockSpec((t, nv), row),
        out_shape=jax.ShapeDtypeStruct((n, nv), BF16),
        scratch_shapes=[pltpu.VMEM((nv, V7X_LANES), F32),
                        pltpu.VMEM((t // CHUNK, nv, V7X_LANES), BF16),
                        pltpu.VMEM((t // CHUNK, nv, V7X_LANES), F32),
                        pltpu.VMEM((t, t), BF16), pltpu.VMEM((t, nv), F32)],
        compiler_params=_cparams(32 << 20),
        name="gla",
    )(q, k, v, r, la, ng)


def _attn_kernel(q_ref, kp_ref, kc_ref, vp_ref, vc_ref, bias_ref, *refs, n_cast):
    src_refs = refs[:n_cast]
    y_ref = refs[n_cast]
    dst_refs = refs[n_cast + 1:2 * n_cast + 1]
    s_ref, e_ref = refs[2 * n_cast + 1:]
    for src, dst in zip(src_refs, dst_refs):
        dst[...] = src[...].astype(dst.dtype)
    t = pl.program_id(1)
    tq = ATT_TILE

    def body(first_step):
        head_of_lane = lax.broadcasted_iota(jnp.int32, (1, ATT_GROUP * ATT_HD), 1) // ATT_HD
        for sub in range(ATT_SUB):
            pieces = []
            for back in (2, 1, 0):
                r0 = (sub - back) * tq
                pieces.append((kp_ref, vp_ref, r0 + ATT_SUB * tq, True) if r0 < 0
                              else (kc_ref, vc_ref, r0, False))
            qrows = slice(sub * tq, (sub + 1) * tq)
            for grp in range(ATT_HEADS // ATT_GROUP):
                cols = slice(grp * ATT_GROUP * ATT_HD, (grp + 1) * ATT_GROUP * ATT_HD)
                qg = q_ref[qrows, cols]
                for hh in range(ATT_GROUP):
                    mine = head_of_lane == hh
                    qm = jnp.where(mine, qg, jnp.zeros_like(qg))
                    head = grp * ATT_GROUP + hh
                    for j, (kr, _, r0, from_prev) in enumerate(pieces):
                        kc = slice(j * tq, (j + 1) * tq)
                        if first_step and from_prev:
                            s_ref[head, :, kc] = jnp.full((tq, tq), NEG_INF, F32)
                        else:
                            s_ref[head, :, kc] = _dot_nt(qm, kr[r0:r0 + tq, cols]) + bias_ref[head, :, kc]
                    e = jnp.exp2(s_ref[head] - jnp.max(s_ref[head], axis=-1, keepdims=True))
                    r = 1.0 / jnp.sum(e, axis=-1, keepdims=True)
                    e_ref[head] = e.astype(BF16)
                    pv = sum(_dot(e_ref[head, :, j * tq:(j + 1) * tq], vr[r0:r0 + tq, cols])
                             for j, (_, vr, r0, _) in enumerate(pieces)) * r
                    hd = slice(hh * ATT_HD, (hh + 1) * ATT_HD)
                    y_ref[qrows, cols.start + hd.start:cols.start + hd.stop] = pv[:, hd].astype(y_ref.dtype)

    pl.when(t == 0)(functools.partial(body, True))
    pl.when(t > 0)(functools.partial(body, False))


def _attn(q, k, v, bias, bsz, to_cast, layer):
    n, w = q.shape
    t = ATT_SUB * ATT_TILE
    nt = n // bsz // t
    steps = bsz * nt
    cur = lambda b, i: (b * nt + i, 0)
    prev = lambda b, i: (b * nt + jnp.maximum(i - 1, 0), 0)
    blk = lambda im: pl.BlockSpec((t, w), im)
    for m in to_cast:
        assert m.shape[1] % (steps * BF16_SUBLANES) == 0, m.shape
    src_slab = lambda m: pl.BlockSpec((None, m.shape[1] // steps, m.shape[2]),
                                      lambda b, i: (layer, b * nt + i, 0))
    dst_slab = lambda m: pl.BlockSpec((m.shape[1] // steps, m.shape[2]), cur)
    outs = pl.pallas_call(
        functools.partial(_attn_kernel, n_cast=len(to_cast)),
        grid=(bsz, nt),
        in_specs=[blk(cur), blk(prev), blk(cur), blk(prev), blk(cur), _const_spec(bias.shape)]
                 + [src_slab(m) for m in to_cast],
        out_specs=[blk(cur)] + [dst_slab(m) for m in to_cast],
        out_shape=[jax.ShapeDtypeStruct((n, w), BF16)]
                  + [jax.ShapeDtypeStruct(m.shape[1:], BF16) for m in to_cast],
        scratch_shapes=[pltpu.VMEM((ATT_HEADS, ATT_TILE, ATT_WIN), F32),
                        pltpu.VMEM((ATT_HEADS, ATT_TILE, ATT_WIN), BF16)],
        compiler_params=_cparams(56 << 20),
        name="attn",
    )(q, k, k, v, v, bias, *to_cast)
    return outs[0], outs[1:]


def _att_bias_table(rel_bias):
    nh, ntab = rel_bias.shape
    period = ATT_TILE + ATT_WIN
    n_lo = (ATT_TILE - 1) - (CHUNK - 1)
    rb = rel_bias.astype(F32)
    vec = jnp.concatenate([jnp.broadcast_to(rb[:, :1], (nh, n_lo)), rb,
                           jnp.broadcast_to(rb[:, -1:], (nh, period - n_lo - ntab))], axis=1)
    rolled = jnp.roll(vec[:, ::-1], -ATT_TILE, axis=1)
    flat = jnp.tile(rolled, (1, ATT_TILE))[:, :ATT_TILE * (period - 1)]
    table = flat.reshape(nh, ATT_TILE, period - 1)[:, :, :ATT_WIN]
    qc = jnp.arange(ATT_TILE)[:, None] // CHUNK + 2 * ATT_TILE // CHUNK
    kc = jnp.arange(ATT_WIN)[None, :] // CHUNK
    in_band = (kc <= qc) & (kc >= qc - (ATT_BAND - 1))
    return jnp.where(in_band[None], table * LOG2E, NEG_INF)


def _conv_tile(x_ref, w_ref, b_ref, lng_ref, lnb_ref, ext_ref, co_ref, y_ref):
    tile, width = x_ref.shape
    nslab = width // V7X_LANES
    out0 = CONV_HALO + tile - 8 * CONV_SEG
    in0 = out0 - (CONV_K - 1)
    lanes = [slice(c * V7X_LANES, (c + 1) * V7X_LANES) for c in range(nslab)]

    @pl.when(pl.program_id(1) == 0)
    def _():
        ext_ref[:, 0:CONV_HALO, :] = jnp.zeros((nslab, CONV_HALO, V7X_LANES), F32)

    for c in range(nslab):
        ext_ref[c, CONV_HALO:, :] = x_ref[:, lanes[c]]

    for c in range(nslab):
        for i0 in range(0, CONV_SEG, CONV_IB):
            accs = [None] * CONV_IB
            for j in range(CONV_K):
                wj = jnp.broadcast_to(w_ref[j:j + 1, lanes[c]], (8, V7X_LANES))
                for a in range(CONV_IB):
                    term = wj * ext_ref[c, pl.ds(in0 + i0 + a + j, 8, stride=CONV_SEG), :]
                    accs[a] = term if accs[a] is None else accs[a] + term
            for a in range(CONV_IB):
                co_ref[c, pl.ds(out0 + i0 + a, 8, stride=CONV_SEG), :] = accs[a]

    for r0 in range(0, tile, CONV_NB):
        parts = [co_ref[c, CONV_HALO + r0:CONV_HALO + r0 + CONV_NB, :] + b_ref[:, lanes[c]]
                 for c in range(nslab)]
        mu = jnp.sum(sum(parts), axis=-1, keepdims=True) / width
        cen = [q - mu for q in parts]
        var = jnp.sum(sum(q * q for q in cen), axis=-1, keepdims=True) / width
        inv = lax.rsqrt(var + EPS)
        for c in range(nslab):
            y = cen[c] * inv * lng_ref[:, lanes[c]] + lnb_ref[:, lanes[c]]
            y_ref[r0:r0 + CONV_NB, lanes[c]] = _silu(y).astype(y_ref.dtype)

    for c in range(nslab):
        ext_ref[c, 0:CONV_HALO, :] = ext_ref[c, tile:tile + CONV_HALO, :]


def _merge_kernel(h_ref, ya_ref, yb_ref, yc_ref, yd_ref, g1_ref, wg_ref, bg_ref, wb_ref, wo_ref, o_ref):
    h = h_ref[...]
    xn = _rms(h, g1_ref[...]).astype(BF16)
    merged = None
    for n, y_ref in enumerate((ya_ref, yb_ref, yc_ref, yd_ref)):
        gate = _sigmoid(_dot(xn, wg_ref[n]) + bg_ref[n])
        term = gate * _dot(y_ref[...], wb_ref[n])
        merged = term if merged is None else merged + term
    o_ref[...] = h + _dot(merged.astype(BF16), wo_ref[...])


def _merge(h, ys, g1, wg, bg, wb, wo):
    n, d = h.shape
    t = DENSE_TILE
    row = lambda i: (i, 0)
    consts = (g1, wg, bg, wb, wo)
    return pl.pallas_call(
        _merge_kernel,
        grid=(n // t,),
        in_specs=[pl.BlockSpec((t, d), row)] + [pl.BlockSpec((t, y.shape[1]), row) for y in ys]
                 + [_const_spec(c.shape) for c in consts],
        out_specs=pl.BlockSpec((t, d), row),
        out_shape=jax.ShapeDtypeStruct((n, d), F32),
        compiler_params=_cparams(58 << 20),
        name="merge",
    )(h, *ys, *consts)


def _ffn_ple_kernel(h_ref, p_ref, g2_ref, w1_ref, w2_ref, g3_ref, wpg_ref, bpg_ref, wp_ref, gf_ref,
                    o_ref, *, final_norm, ff_chunk):
    h = h_ref[...]
    hn = _rms(h, g2_ref[...]).astype(BF16)
    acc = h
    for j in range(w1_ref.shape[1] // ff_chunk):
        cols = slice(j * ff_chunk, (j + 1) * ff_chunk)
        a = jnp.maximum(_dot(hn, w1_ref[:, cols]), 0.0)
        acc = acc + _dot((a * a).astype(BF16), w2_ref[cols, :])
    hg = _rms(acc, g3_ref[...]).astype(BF16)
    gate = _sigmoid(_dot(hg, wpg_ref[...]) + bpg_ref[...])
    out = acc + gate * _dot(p_ref[...].astype(BF16), wp_ref[...])
    if final_norm:
        out = _rms(out, gf_ref[...])
    o_ref[...] = out


def _ffn_ple(h, p, layer, g2, w1, w2, g3, wpg, bpg, wp, gf, final_norm):
    n, d = h.shape
    t = DENSE_TILE
    row = lambda i: (i, 0)
    consts = (g2, w1, w2, g3, wpg, bpg, wp, gf)
    return pl.pallas_call(
        functools.partial(_ffn_ple_kernel, final_norm=final_norm, ff_chunk=1024),
        grid=(n // t,),
        in_specs=[pl.BlockSpec((t, d), row),
                  pl.BlockSpec((None, t, p.shape[2]), lambda i: (layer, i, 0))]
                 + [_const_spec(c.shape) for c in consts],
        out_specs=pl.BlockSpec((t, d), row),
        out_shape=jax.ShapeDtypeStruct((n, d), F32),
        compiler_params=_cparams(58 << 20),
        name="ffn_ple",
    )(h, p, *consts)


def kernel(x, p, norm1_g, w_in, sg_ln_g, sg_ln_b, sg_w, sg_b, gla_w_a2, gla_b_a, gla_norm_g, att_rel_bias, conv_dw_w, conv_dw_b, conv_ln_g, conv_ln_b, w_branch, w_gate, b_gate, w_out, norm2_g, w_ff1, w_ff2, norm3_g, w_ple_gate, b_ple_gate, w_ple, final_g):
    bsz, seq, d = x.shape
    depth = w_in.shape[0]
    n = bsz * seq
    bw = d // 2
    nqk = GLA_HEADS * GLA_DK
    assert seq % TOK_TILE == 0 and seq % (ATT_SUB * ATT_TILE) == 0 and n % DENSE_TILE == 0
    assert bw == SG_GROUPS * V7X_LANES
    row2 = lambda a: a.reshape(1, -1)

    c_ga = 2 * bw + 2 * nqk + 2 * bw
    c_att = c_ga + GLA_RANK

    h = x.reshape(n, d)
    p3 = p.reshape(depth, n, -1)
    for i in range(depth):
        w = jnp.concatenate([w_in[i, :, :c_ga], w_in[i, :, c_att:],
                             jnp.pad(w_in[i, :, c_ga:c_att], ((0, 0), (0, V7X_LANES - GLA_RANK)))],
                            axis=1).astype(BF16)
        wa2 = jnp.pad(gla_w_a2[i].astype(BF16), ((0, V7X_LANES - GLA_RANK), (0, 0)))
        sgbias = jnp.repeat(sg_b[i].T, bw // SG_GROUPS, axis=1)
        g1 = row2(norm1_g[i])
        (y_a, gq, gk, gv, gr, la, aq, ak, av, y_d) = _in_proj(
            h, g1, w, row2(sg_ln_g[i]), row2(sg_ln_b[i]), sg_w[i], sgbias, wa2, row2(gla_b_a[i]),
            jnp.pad(conv_dw_w[i], ((0, 1), (0, 0))), row2(conv_dw_b[i]),
            row2(conv_ln_g[i]), row2(conv_ln_b[i]), bsz)
        y_b = _gla(gq, gk, gv, gr, la, row2(gla_norm_g[i]), bsz)
        y_c, (wg, wb, wo, w1, w2, wpg) = _attn(
            aq, ak, av, _att_bias_table(att_rel_bias[i]), bsz,
            (w_gate.reshape(depth, -1, d), w_branch.reshape(depth, -1, d), w_out, w_ff1, w_ff2,
             w_ple_gate), i)
        h = _merge(h, (y_a, y_b, y_c, y_d), g1, wg.reshape(w_gate.shape[1:]), b_gate[i][:, None, :],
                   wb.reshape(w_branch.shape[1:]), wo)
        h = _ffn_ple(h, p3, i, row2(norm2_g[i]), w1, w2, row2(norm3_g[i]), wpg,
                     row2(b_ple_gate[i]), w_ple[i].astype(BF16), row2(final_g),
                     final_norm=(i == depth - 1))
    return h.reshape(bsz, seq, d)
```

```python
import functools

import jax
import jax.numpy as jnp
from jax import lax
from jax.experimental import pallas as pl
from jax.experimental.pallas import tpu as pltpu

F32 = jnp.float32
BF16 = jnp.bfloat16

EPS = 1e-6
NEG_INF = -1e30
LOG2E = 1.4426950408889634

CHUNK = 64
SG_BLOCK = 128
SG_GROUPS = 4
GLA_HEADS = 4
GLA_DK = 64
GLA_DV = 128
GLA_RANK = 16
GLA_TAU = 16.0
ATT_HEADS = 8
ATT_HD = 64
ATT_BAND = 9
MAX_REL = 256
CONV_K = 31

V7X_LANES = 128
BF16_SUBLANES = 16
V7X_VMEM_BYTES = 64 * 1024 * 1024

TOK_TILE = 512
GLA_TILE = 1024
GLA_SUFFIX_ROWS = 512
DENSE_TILE = 1024
ATT_TILE = 256
ATT_WIN = 3 * ATT_TILE
ATT_GROUP = 4
ATT_SUB = 4
CONV_HALO = 64
CONV_SEG = 68
CONV_IB = 17
CONV_NB = 64


def _cparams(vmem_bytes):
    return pltpu.CompilerParams(
        dimension_semantics=None,
        vmem_limit_bytes=int(min(vmem_bytes, V7X_VMEM_BYTES - (6 << 20))),
    )


def _const_spec(shape):
    nd = len(shape)
    return pl.BlockSpec(shape, lambda *_: (0,) * nd, pipeline_mode=pl.Buffered(1))


def _rms(x, g):
    ms = jnp.mean(x * x, axis=-1, keepdims=True)
    return x * lax.rsqrt(ms + EPS) * g


def _layer_norm(x, g, b):
    mu = jnp.mean(x, axis=-1, keepdims=True)
    xc = x - mu
    var = jnp.mean(xc * xc, axis=-1, keepdims=True)
    return xc * lax.rsqrt(var + EPS) * g + b


def _dot(a, b):
    return jnp.dot(a, b, preferred_element_type=F32)


def _dot_nt(a, b):
    return lax.dot_general(a, b, (((1,), (1,)), ((), ())), preferred_element_type=F32)


def _dot_tn(a, b):
    return lax.dot_general(a, b, (((0,), (0,)), ((), ())), preferred_element_type=F32)


def _sigmoid(x):
    return 0.5 * jnp.tanh(0.5 * x) + 0.5


def _silu(x):
    return x * _sigmoid(x)


def _log_sigmoid(x):
    return jnp.minimum(x, 0.0) - jnp.log(1.0 + jnp.exp(-jnp.abs(x)))


def _inproj_kernel(h_ref, g1_ref, w_ref,
                   lng_ref, lnb_ref, sgw_ref, sgbias_ref, wa2_ref, ba_ref,
                   cw_ref, cb_ref, clg_ref, clb_ref,
                   ya_ref, gq_ref, gk_ref, gv_ref, gr_ref, la_ref,
                   aq_ref, ak_ref, av_ref, yd_ref, cy_ref, ext_ref, co_ref):
    tile = h_ref.shape[0]
    xn = _rms(h_ref[...], g1_ref[...]).astype(BF16)
    bw = ya_ref.shape[1]
    o_gla = 2 * bw
    o_att = o_gla + 2 * GLA_HEADS * GLA_DK + 2 * bw
    o_cv = o_att + 3 * bw
    o_ga = o_cv + 2 * bw

    pc = _dot(xn, w_ref[:, o_cv:o_ga])
    cw = pc.shape[1] // 2
    cy_ref[...] = pc[:, :cw] * _sigmoid(pc[:, cw:])
    _conv_tile(cy_ref, cw_ref, cb_ref, clg_ref, clb_ref, ext_ref, co_ref, yd_ref)

    ps = _dot(xn, w_ref[:, 0:o_gla])
    half = ps.shape[1] // 2
    u = jax.nn.gelu(ps[:, :half])
    vn = _layer_norm(jax.nn.gelu(ps[:, half:]), lng_ref[...], lnb_ref[...]).astype(BF16)
    pi = lax.broadcasted_iota(jnp.int32, (SG_BLOCK, SG_BLOCK), 0) // CHUNK
    pj = lax.broadcasted_iota(jnp.int32, (SG_BLOCK, SG_BLOCK), 1) // CHUNK
    cg = half // SG_GROUPS
    nblk = tile // SG_BLOCK
    for g in range(SG_GROUPS):
        wm = jnp.where(pj <= pi, sgw_ref[g], 0.0).astype(BF16)
        cols = slice(g * cg, (g + 1) * cg)
        vblocks = jnp.concatenate([vn[n * SG_BLOCK:(n + 1) * SG_BLOCK, cols] for n in range(nblk)], axis=1)
        mixed = _dot(wm, vblocks)
        for n in range(nblk):
            rows = slice(n * SG_BLOCK, (n + 1) * SG_BLOCK)
            ya_ref[rows, cols] = (u[rows, cols] * (mixed[:, n * cg:(n + 1) * cg] + sgbias_ref[:, cols])
                                  ).astype(ya_ref.dtype)

    pg = _dot(xn, w_ref[:, o_gla:o_att])
    nqk = GLA_HEADS * GLA_DK
    nv = GLA_HEADS * GLA_DV
    gq_ref[...] = (pg[:, :nqk] * (GLA_DK ** -0.5)).astype(gq_ref.dtype)
    gk_ref[...] = pg[:, nqk:2 * nqk]
    gv_ref[...] = pg[:, 2 * nqk:2 * nqk + nv].astype(gv_ref.dtype)
    gr_ref[...] = pg[:, 2 * nqk + nv:]
    a_lr = _dot(xn, w_ref[:, o_ga:]).astype(BF16)
    la_ref[...] = _log_sigmoid(_dot(a_lr, wa2_ref[...]) + ba_ref[...]) / GLA_TAU

    pa = _dot(xn, w_ref[:, o_att:o_cv])
    na = ATT_HEADS * ATT_HD
    aq_ref[...] = (pa[:, :na] * (ATT_HD ** -0.5 * LOG2E)).astype(aq_ref.dtype)
    ak_ref[...] = pa[:, na:2 * na].astype(ak_ref.dtype)
    av_ref[...] = pa[:, 2 * na:].astype(av_ref.dtype)


def _in_proj(h, g1, w, lng, lnb, sgw, sgbias, wa2, ba, cw, cb, clg, clb, bsz):
    n, d = h.shape
    t = TOK_TILE
    nt = n // bsz // t
    bw = d // 2
    nqk = GLA_HEADS * GLA_DK
    assert CONV_HALO + t >= 8 * CONV_SEG + CONV_K - 1 and 8 * CONV_SEG >= t
    assert CONV_SEG % CONV_IB == 0 and CONV_SEG % 8 == 4
    row = lambda b, i: (b * nt + i, 0)
    slabs = (bw // V7X_LANES, t + CONV_HALO, V7X_LANES)
    outs = [
        ((n, bw), BF16),
        ((n, nqk), BF16),
        ((n, nqk), F32),
        ((n, bw), BF16),
        ((n, bw), F32),
        ((n, nqk), F32),
        ((n, bw), BF16),
        ((n, bw), BF16),
        ((n, bw), BF16),
        ((n, bw), BF16),
    ]
    consts = (g1, w, lng, lnb, sgw, sgbias, wa2, ba, cw, cb, clg, clb)
    return pl.pallas_call(
        _inproj_kernel,
        grid=(bsz, nt),
        in_specs=[pl.BlockSpec((t, d), row)] + [_const_spec(c.shape) for c in consts],
        out_specs=[pl.BlockSpec((t, s[1]), row) for s, _ in outs],
        out_shape=[jax.ShapeDtypeStruct(s, dt) for s, dt in outs],
        scratch_shapes=[pltpu.VMEM((t, bw), F32), pltpu.VMEM(slabs, F32), pltpu.VMEM(slabs, F32)],
        compiler_params=_cparams(56 << 20),
        name="in_proj",
    )(h, *consts)


def _gla_kernel(q_ref, k_ref, v_ref, r_ref, la_ref, ng_ref, y_ref,
                st_ref, stb_ref, upd_ref, later_ref, o_ref):
    tile = q_ref.shape[0]
    nchunk = tile // CHUNK
    pair_rows = 2 * GLA_DV
    npair = GLA_HEADS // 2

    sb = later_ref.shape[0]

    @pl.when(pl.program_id(1) == 0)
    def _():
        st_ref[...] = jnp.zeros_like(st_ref)
        ri = lax.broadcasted_iota(jnp.int32, (sb, sb), 0)
        ci = lax.broadcasted_iota(jnp.int32, (sb, sb), 1)
        later_ref[...] = jnp.where((ci > ri) & (ci // CHUNK == ri // CHUNK), 1.0, 0.0).astype(BF16)

    la = la_ref[...]
    hi = la.astype(BF16)
    lo = (la - hi.astype(F32)).astype(BF16)
    sfx = jnp.concatenate([_dot(later_ref[...], hi[r0:r0 + sb]) + _dot(later_ref[...], lo[r0:r0 + sb])
                           for r0 in range(0, tile, sb)], axis=0)
    kd = (k_ref[...] * jnp.exp(sfx)).astype(BF16)
    total = la + sfx

    own_lanes = (lax.broadcasted_iota(jnp.int32, (pair_rows, V7X_LANES), 1) // GLA_DK
                 == lax.broadcasted_iota(jnp.int32, (pair_rows, V7X_LANES), 0) // GLA_DV)

    for c in range(nchunk):
        rows = slice(c * CHUNK, (c + 1) * CHUNK)
        for p in range(npair):
            upd = _dot_tn(v_ref[rows, p * pair_rows:(p + 1) * pair_rows],
                          kd[rows, p * V7X_LANES:(p + 1) * V7X_LANES])
            upd_ref[c, p * pair_rows:(p + 1) * pair_rows, :] = jnp.where(own_lanes, upd, 0.0)

    st = [st_ref[p * pair_rows:(p + 1) * pair_rows, :] for p in range(npair)]
    for c in range(nchunk):
        decay = jnp.exp(total[c * CHUNK:c * CHUNK + 1, :])
        for p in range(npair):
            prow = slice(p * pair_rows, (p + 1) * pair_rows)
            st[p] = st[p] * decay[:, p * V7X_LANES:(p + 1) * V7X_LANES] + upd_ref[c, prow, :]
            stb_ref[c, prow, :] = st[p].astype(BF16)
    for p in range(npair):
        st_ref[p * pair_rows:(p + 1) * pair_rows, :] = st[p]

    for c in range(nchunk):
        rows = slice(c * CHUNK, (c + 1) * CHUNK)
        for p in range(npair):
            prow = slice(p * pair_rows, (p + 1) * pair_rows)
            o_ref[rows, prow] = _dot_nt(q_ref[rows, p * V7X_LANES:(p + 1) * V7X_LANES], stb_ref[c, prow, :])

    for h in range(GLA_HEADS):
        cols = slice(h * GLA_DV, (h + 1) * GLA_DV)
        o = o_ref[:, cols]
        o = o * lax.rsqrt(jnp.mean(o * o, axis=-1, keepdims=True) + EPS) * ng_ref[:, cols]
        y_ref[:, cols] = (o * _silu(r_ref[:, cols])).astype(y_ref.dtype)


def _gla(q, k, v, r, la, ng, bsz):
    n = q.shape[0]
    t = GLA_TILE
    nt = n // bsz // t
    nqk = GLA_HEADS * GLA_DK
    nv = GLA_HEADS * GLA_DV
    row = lambda b, i: (b * nt + i, 0)
    return pl.pallas_call(
        _gla_kernel,
        grid=(bsz, nt),
        in_specs=[pl.BlockSpec((t, nqk), row), pl.BlockSpec((t, nqk), row),
                  pl.BlockSpec((t, nv), row), pl.BlockSpec((t, nv), row),
                  pl.BlockSpec((t, nqk), row), _const_spec(ng.shape)],
        out_specs=pl.BlockSpec((t, nv), row),
        out_shape=jax.ShapeDtypeStruct((n, nv), BF16),
        scratch_shapes=[pltpu.VMEM((nv, V7X_LANES), F32),
                        pltpu.VMEM((t // CHUNK, nv, V7X_LANES), BF16),
                        pltpu.VMEM((t // CHUNK, nv, V7X_LANES), F32),
                        pltpu.VMEM((GLA_SUFFIX_ROWS, GLA_SUFFIX_ROWS), BF16), pltpu.VMEM((t, nv), F32)],
        compiler_params=_cparams(40 << 20),
        name="gla",
    )(q, k, v, r, la, ng)


def _attn_kernel(q_ref, kp_ref, kc_ref, vp_ref, vc_ref, bias_ref, *refs, n_cast):
    src_refs = refs[:n_cast]
    y_ref = refs[n_cast]
    dst_refs = refs[n_cast + 1:2 * n_cast + 1]
    s_ref, e_ref = refs[2 * n_cast + 1:]
    for src, dst in zip(src_refs, dst_refs):
        dst[...] = src[...].astype(dst.dtype)
    t = pl.program_id(1)
    tq = ATT_TILE

    def body(first_step):
        head_of_lane = lax.broadcasted_iota(jnp.int32, (1, ATT_GROUP * ATT_HD), 1) // ATT_HD
        for sub in range(ATT_SUB):
            pieces = []
            for back in (2, 1, 0):
                r0 = (sub - back) * tq
                pieces.append((kp_ref, vp_ref, r0 + ATT_SUB * tq, True) if r0 < 0
                              else (kc_ref, vc_ref, r0, False))
            qrows = slice(sub * tq, (sub + 1) * tq)
            for grp in range(ATT_HEADS // ATT_GROUP):
                cols = slice(grp * ATT_GROUP * ATT_HD, (grp + 1) * ATT_GROUP * ATT_HD)
                qg = q_ref[qrows, cols]
                for hh in range(ATT_GROUP):
                    mine = head_of_lane == hh
                    qm = jnp.where(mine, qg, jnp.zeros_like(qg))
                    head = grp * ATT_GROUP + hh
                    for j, (kr, _, r0, from_prev) in enumerate(pieces):
                        kc = slice(j * tq, (j + 1) * tq)
                        if first_step and from_prev:
                            s_ref[head, :, kc] = jnp.full((tq, tq), NEG_INF, F32)
                        else:
                            s_ref[head, :, kc] = _dot_nt(qm, kr[r0:r0 + tq, cols]) + bias_ref[head, :, kc]
                    e = jnp.exp2(s_ref[head] - jnp.max(s_ref[head], axis=-1, keepdims=True))
                    r = 1.0 / jnp.sum(e, axis=-1, keepdims=True)
                    e_ref[head] = e.astype(BF16)
                    pv = sum(_dot(e_ref[head, :, j * tq:(j + 1) * tq], vr[r0:r0 + tq, cols])
                             for j, (_, vr, r0, _) in enumerate(pieces)) * r
                    hd = slice(hh * ATT_HD, (hh + 1) * ATT_HD)
                    y_ref[qrows, cols.start + hd.start:cols.start + hd.stop] = pv[:, hd].astype(y_ref.dtype)

    pl.when(t == 0)(functools.partial(body, True))
    pl.when(t > 0)(functools.partial(body, False))


def _attn(q, k, v, bias, bsz, to_cast, layer):
    n, w = q.shape
    t = ATT_SUB * ATT_TILE
    nt = n // bsz // t
    steps = bsz * nt
    cur = lambda b, i: (b * nt + i, 0)
    prev = lambda b, i: (b * nt + jnp.maximum(i - 1, 0), 0)
    blk = lambda im: pl.BlockSpec((t, w), im)
    for m in to_cast:
        assert m.shape[1] % (steps * BF16_SUBLANES) == 0, m.shape
    src_slab = lambda m: pl.BlockSpec((None, m.shape[1] // steps, m.shape[2]),
                                      lambda b, i: (layer, b * nt + i, 0))
    dst_slab = lambda m: pl.BlockSpec((m.shape[1] // steps, m.shape[2]), cur)
    outs = pl.pallas_call(
        functools.partial(_attn_kernel, n_cast=len(to_cast)),
        grid=(bsz, nt),
        in_specs=[blk(cur), blk(prev), blk(cur), blk(prev), blk(cur), _const_spec(bias.shape)]
                 + [src_slab(m) for m in to_cast],
        out_specs=[blk(cur)] + [dst_slab(m) for m in to_cast],
        out_shape=[jax.ShapeDtypeStruct((n, w), BF16)]
                  + [jax.ShapeDtypeStruct(m.shape[1:], BF16) for m in to_cast],
        scratch_shapes=[pltpu.VMEM((ATT_HEADS, ATT_TILE, ATT_WIN), F32),
                        pltpu.VMEM((ATT_HEADS, ATT_TILE, ATT_WIN), BF16)],
        compiler_params=_cparams(56 << 20),
        name="attn",
    )(q, k, k, v, v, bias, *to_cast)
    return outs[0], outs[1:]


def _att_bias_table(rel_bias):
    nh, ntab = rel_bias.shape
    period = ATT_TILE + ATT_WIN
    n_lo = (ATT_TILE - 1) - (CHUNK - 1)
    rb = rel_bias.astype(F32)
    vec = jnp.concatenate([jnp.broadcast_to(rb[:, :1], (nh, n_lo)), rb,
                           jnp.broadcast_to(rb[:, -1:], (nh, period - n_lo - ntab))], axis=1)
    rolled = jnp.roll(vec[:, ::-1], -ATT_TILE, axis=1)
    flat = jnp.tile(rolled, (1, ATT_TILE))[:, :ATT_TILE * (period - 1)]
    table = flat.reshape(nh, ATT_TILE, period - 1)[:, :, :ATT_WIN]
    qc = jnp.arange(ATT_TILE)[:, None] // CHUNK + 2 * ATT_TILE // CHUNK
    kc = jnp.arange(ATT_WIN)[None, :] // CHUNK
    in_band = (kc <= qc) & (kc >= qc - (ATT_BAND - 1))
    return jnp.where(in_band[None], table * LOG2E, NEG_INF)


def _conv_tile(x_ref, w_ref, b_ref, lng_ref, lnb_ref, ext_ref, co_ref, y_ref):
    tile, width = x_ref.shape
    nslab = width // V7X_LANES
    out0 = CONV_HALO + tile - 8 * CONV_SEG
    in0 = out0 - (CONV_K - 1)
    lanes = [slice(c * V7X_LANES, (c + 1) * V7X_LANES) for c in range(nslab)]

    @pl.when(pl.program_id(1) == 0)
    def _():
        ext_ref[:, 0:CONV_HALO, :] = jnp.zeros((nslab, CONV_HALO, V7X_LANES), F32)

    for c in range(nslab):
        ext_ref[c, CONV_HALO:, :] = x_ref[:, lanes[c]]

    for c in range(nslab):
        for i0 in range(0, CONV_SEG, CONV_IB):
            accs = [None] * CONV_IB
            for j in range(CONV_K):
                wj = jnp.broadcast_to(w_ref[j:j + 1, lanes[c]], (8, V7X_LANES))
                for a in range(CONV_IB):
                    term = wj * ext_ref[c, pl.ds(in0 + i0 + a + j, 8, stride=CONV_SEG), :]
                    accs[a] = term if accs[a] is None else accs[a] + term
            for a in range(CONV_IB):
                co_ref[c, pl.ds(out0 + i0 + a, 8, stride=CONV_SEG), :] = accs[a]

    for r0 in range(0, tile, CONV_NB):
        parts = [co_ref[c, CONV_HALO + r0:CONV_HALO + r0 + CONV_NB, :] + b_ref[:, lanes[c]]
                 for c in range(nslab)]
        mu = jnp.sum(sum(parts), axis=-1, keepdims=True) / width
        cen = [q - mu for q in parts]
        var = jnp.sum(sum(q * q for q in cen), axis=-1, keepdims=True) / width
        inv = lax.rsqrt(var + EPS)
        for c in range(nslab):
            y = cen[c] * inv * lng_ref[:, lanes[c]] + lnb_ref[:, lanes[c]]
            y_ref[r0:r0 + CONV_NB, lanes[c]] = _silu(y).astype(y_ref.dtype)

    for c in range(nslab):
        ext_ref[c, 0:CONV_HALO, :] = ext_ref[c, tile:tile + CONV_HALO, :]


def _merge_kernel(h_ref, ya_ref, yb_ref, yc_ref, yd_ref, g1_ref, wg_ref, bg_ref, wb_ref, wo_ref, o_ref):
    h = h_ref[...]
    xn = _rms(h, g1_ref[...]).astype(BF16)
    merged = None
    for n, y_ref in enumerate((ya_ref, yb_ref, yc_ref, yd_ref)):
        gate = _sigmoid(_dot(xn, wg_ref[n]) + bg_ref[n])
        term = gate * _dot(y_ref[...], wb_ref[n])
        merged = term if merged is None else merged + term
    o_ref[...] = h + _dot(merged.astype(BF16), wo_ref[...])


def _merge(h, ys, g1, wg, bg, wb, wo):
    n, d = h.shape
    t = DENSE_TILE
    row = lambda i: (i, 0)
    consts = (g1, wg, bg, wb, wo)
    return pl.pallas_call(
        _merge_kernel,
        grid=(n // t,),
        in_specs=[pl.BlockSpec((t, d), row)] + [pl.BlockSpec((t, y.shape[1]), row) for y in ys]
                 + [_const_spec(c.shape) for c in consts],
        out_specs=pl.BlockSpec((t, d), row),
        out_shape=jax.ShapeDtypeStruct((n, d), F32),
        compiler_params=_cparams(58 << 20),
        name="merge",
    )(h, *ys, *consts)


def _ffn_ple_kernel(h_ref, p_ref, g2_ref, w1_ref, w2_ref, g3_ref, wpg_ref, bpg_ref, wp_ref, gf_ref,
                    o_ref, *, final_norm, ff_chunk):
    h = h_ref[...]
    hn = _rms(h, g2_ref[...]).astype(BF16)
    acc = h
    for j in range(w1_ref.shape[1] // ff_chunk):
        cols = slice(j * ff_chunk, (j + 1) * ff_chunk)
        a = jnp.maximum(_dot(hn, w1_ref[:, cols]), 0.0)
        acc = acc + _dot((a * a).astype(BF16), w2_ref[cols, :])
    hg = _rms(acc, g3_ref[...]).astype(BF16)
    gate = _sigmoid(_dot(hg, wpg_ref[...]) + bpg_ref[...])
    out = acc + gate * _dot(p_ref[...].astype(BF16), wp_ref[...])
    if final_norm:
        out = _rms(out, gf_ref[...])
    o_ref[...] = out


def _ffn_ple(h, p, layer, g2, w1, w2, g3, wpg, bpg, wp, gf, final_norm):
    n, d = h.shape
    t = DENSE_TILE
    row = lambda i: (i, 0)
    consts = (g2, w1, w2, g3, wpg, bpg, wp, gf)
    return pl.pallas_call(
        functools.partial(_ffn_ple_kernel, final_norm=final_norm, ff_chunk=1024),
        grid=(n // t,),
        in_specs=[pl.BlockSpec((t, d), row),
                  pl.BlockSpec((None, t, p.shape[2]), lambda i: (layer, i, 0))]
                 + [_const_spec(c.shape) for c in consts],
        out_specs=pl.BlockSpec((t, d), row),
        out_shape=jax.ShapeDtypeStruct((n, d), F32),
        compiler_params=_cparams(58 << 20),
        name="ffn_ple",
    )(h, p, *consts)


def kernel(x, p, norm1_g, w_in, sg_ln_g, sg_ln_b, sg_w, sg_b, gla_w_a2, gla_b_a, gla_norm_g, att_rel_bias, conv_dw_w, conv_dw_b, conv_ln_g, conv_ln_b, w_branch, w_gate, b_gate, w_out, norm2_g, w_ff1, w_ff2, norm3_g, w_ple_gate, b_ple_gate, w_ple, final_g):
    bsz, seq, d = x.shape
    depth = w_in.shape[0]
    n = bsz * seq
    bw = d // 2
    nqk = GLA_HEADS * GLA_DK
    assert seq % TOK_TILE == 0 and seq % (ATT_SUB * ATT_TILE) == 0 and n % DENSE_TILE == 0
    assert seq % GLA_TILE == 0 and GLA_TILE % GLA_SUFFIX_ROWS == 0 and GLA_SUFFIX_ROWS % CHUNK == 0
    assert bw == SG_GROUPS * V7X_LANES
    row2 = lambda a: a.reshape(1, -1)

    c_ga = 2 * bw + 2 * nqk + 2 * bw
    c_att = c_ga + GLA_RANK

    h = x.reshape(n, d)
    p3 = p.reshape(depth, n, -1)
    for i in range(depth):
        w = jnp.concatenate([w_in[i, :, :c_ga], w_in[i, :, c_att:],
                             jnp.pad(w_in[i, :, c_ga:c_att], ((0, 0), (0, V7X_LANES - GLA_RANK)))],
                            axis=1).astype(BF16)
        wa2 = jnp.pad(gla_w_a2[i].astype(BF16), ((0, V7X_LANES - GLA_RANK), (0, 0)))
        sgbias = jnp.repeat(sg_b[i].T, bw // SG_GROUPS, axis=1)
        g1 = row2(norm1_g[i])
        (y_a, gq, gk, gv, gr, la, aq, ak, av, y_d) = _in_proj(
            h, g1, w, row2(sg_ln_g[i]), row2(sg_ln_b[i]), sg_w[i], sgbias, wa2, row2(gla_b_a[i]),
            jnp.pad(conv_dw_w[i], ((0, 1), (0, 0))), row2(conv_dw_b[i]),
            row2(conv_ln_g[i]), row2(conv_ln_b[i]), bsz)
        y_b = _gla(gq, gk, gv, gr, la, row2(gla_norm_g[i]), bsz)
        y_c, (wg, wb, wo, w1, w2, wpg) = _attn(
            aq, ak, av, _att_bias_table(att_rel_bias[i]), bsz,
            (w_gate.reshape(depth, -1, d), w_branch.reshape(depth, -1, d), w_out, w_ff1, w_ff2,
             w_ple_gate), i)
        h = _merge(h, (y_a, y_b, y_c, y_d), g1, wg.reshape(w_gate.shape[1:]), b_gate[i][:, None, :],
                   wb.reshape(w_branch.shape[1:]), wo)
        h = _ffn_ple(h, p3, i, row2(norm2_g[i]), w1, w2, row2(norm3_g[i]), wpg,
                     row2(b_ple_gate[i]), w_ple[i].astype(BF16), row2(final_g),
                     final_norm=(i == depth - 1))
    return h.reshape(bsz, seq, d)
```

```python
import functools

import jax
import jax.numpy as jnp
from jax import lax
from jax.experimental import pallas as pl
from jax.experimental.pallas import tpu as pltpu

F32 = jnp.float32
BF16 = jnp.bfloat16

EPS = 1e-6
NEG_INF = -1e30
LOG2E = 1.4426950408889634

CHUNK = 64
SG_BLOCK = 128
SG_GROUPS = 4
GLA_HEADS = 4
GLA_DK = 64
GLA_DV = 128
GLA_RANK = 16
GLA_TAU = 16.0
ATT_HEADS = 8
ATT_HD = 64
ATT_BAND = 9
MAX_REL = 256
CONV_K = 31

V7X_LANES = 128
BF16_SUBLANES = 16
V7X_VMEM_BYTES = 64 * 1024 * 1024

TOK_TILE = 512
GLA_TILE = 1024
GLA_SUFFIX_ROWS = 512
DENSE_TILE = 1024
ATT_TILE = 256
ATT_WIN = 3 * ATT_TILE
ATT_GROUP = 4
ATT_SUB = 4
CONV_HALO = 64
CONV_SEG = 68
CONV_IB = 17
CONV_NB = 64


def _cparams(vmem_bytes):
    return pltpu.CompilerParams(
        dimension_semantics=None,
        vmem_limit_bytes=int(min(vmem_bytes, V7X_VMEM_BYTES - (6 << 20))),
    )


def _const_spec(shape):
    nd = len(shape)
    return pl.BlockSpec(shape, lambda *_: (0,) * nd, pipeline_mode=pl.Buffered(1))


def _rms(x, g):
    ms = jnp.mean(x * x, axis=-1, keepdims=True)
    return x * lax.rsqrt(ms + EPS) * g


def _layer_norm(x, g, b):
    mu = jnp.mean(x, axis=-1, keepdims=True)
    xc = x - mu
    var = jnp.mean(xc * xc, axis=-1, keepdims=True)
    return xc * lax.rsqrt(var + EPS) * g + b


def _dot(a, b):
    return jnp.dot(a, b, preferred_element_type=F32)


def _dot_nt(a, b):
    return lax.dot_general(a, b, (((1,), (1,)), ((), ())), preferred_element_type=F32)


def _dot_tn(a, b):
    return lax.dot_general(a, b, (((0,), (0,)), ((), ())), preferred_element_type=F32)


def _sigmoid(x):
    return 0.5 * jnp.tanh(0.5 * x) + 0.5


def _silu(x):
    return x * _sigmoid(x)


def _log_sigmoid(x):
    return jnp.minimum(x, 0.0) - jnp.log(1.0 + jnp.exp(-jnp.abs(x)))


def _inproj_kernel(h_ref, g1_ref, w_ref,
                   lng_ref, lnb_ref, sgw_ref, sgbias_ref, wa2_ref, ba_ref,
                   cw_ref, cb_ref, clg_ref, clb_ref,
                   ya_ref, gq_ref, gk_ref, gv_ref, gr_ref, la_ref,
                   aq_ref, ak_ref, av_ref, yd_ref, cy_ref, ext_ref, co_ref):
    tile = h_ref.shape[0]
    xn = _rms(h_ref[...], g1_ref[...]).astype(BF16)
    bw = ya_ref.shape[1]
    o_gla = 2 * bw
    o_att = o_gla + 2 * GLA_HEADS * GLA_DK + 2 * bw
    o_cv = o_att + 3 * bw
    o_ga = o_cv + 2 * bw

    pc = _dot(xn, w_ref[:, o_cv:o_ga])
    cw = pc.shape[1] // 2
    cy_ref[...] = pc[:, :cw] * _sigmoid(pc[:, cw:])
    _conv_tile(cy_ref, cw_ref, cb_ref, clg_ref, clb_ref, ext_ref, co_ref, yd_ref)

    ps = _dot(xn, w_ref[:, 0:o_gla])
    half = ps.shape[1] // 2
    u = jax.nn.gelu(ps[:, :half])
    vn = _layer_norm(jax.nn.gelu(ps[:, half:]), lng_ref[...], lnb_ref[...]).astype(BF16)
    pi = lax.broadcasted_iota(jnp.int32, (SG_BLOCK, SG_BLOCK), 0) // CHUNK
    pj = lax.broadcasted_iota(jnp.int32, (SG_BLOCK, SG_BLOCK), 1) // CHUNK
    cg = half // SG_GROUPS
    nblk = tile // SG_BLOCK
    for g in range(SG_GROUPS):
        wm = jnp.where(pj <= pi, sgw_ref[g], 0.0).astype(BF16)
        cols = slice(g * cg, (g + 1) * cg)
        vblocks = jnp.concatenate([vn[n * SG_BLOCK:(n + 1) * SG_BLOCK, cols] for n in range(nblk)], axis=1)
        mixed = _dot(wm, vblocks)
        for n in range(nblk):
            rows = slice(n * SG_BLOCK, (n + 1) * SG_BLOCK)
            ya_ref[rows, cols] = (u[rows, cols] * (mixed[:, n * cg:(n + 1) * cg] + sgbias_ref[:, cols])
                                  ).astype(ya_ref.dtype)

    pg = _dot(xn, w_ref[:, o_gla:o_att])
    nqk = GLA_HEADS * GLA_DK
    nv = GLA_HEADS * GLA_DV
    gq_ref[...] = (pg[:, :nqk] * (GLA_DK ** -0.5)).astype(gq_ref.dtype)
    gk_ref[...] = pg[:, nqk:2 * nqk]
    gv_ref[...] = pg[:, 2 * nqk:2 * nqk + nv].astype(gv_ref.dtype)
    gr_ref[...] = pg[:, 2 * nqk + nv:]
    a_lr = _dot(xn, w_ref[:, o_ga:]).astype(BF16)
    la_ref[...] = _log_sigmoid(_dot(a_lr, wa2_ref[...]) + ba_ref[...]) / GLA_TAU

    pa = _dot(xn, w_ref[:, o_att:o_cv])
    na = ATT_HEADS * ATT_HD
    aq_ref[...] = (pa[:, :na] * (ATT_HD ** -0.5 * LOG2E)).astype(aq_ref.dtype)
    ak_ref[...] = pa[:, na:2 * na].astype(ak_ref.dtype)
    av_ref[...] = pa[:, 2 * na:].astype(av_ref.dtype)


def _in_proj(h, g1, w, lng, lnb, sgw, sgbias, wa2, ba, cw, cb, clg, clb, bsz):
    n, d = h.shape
    t = TOK_TILE
    nt = n // bsz // t
    bw = d // 2
    nqk = GLA_HEADS * GLA_DK
    assert CONV_HALO + t >= 8 * CONV_SEG + CONV_K - 1 and 8 * CONV_SEG >= t
    assert CONV_SEG % CONV_IB == 0 and CONV_SEG % 8 == 4
    row = lambda b, i: (b * nt + i, 0)
    slabs = (bw // V7X_LANES, t + CONV_HALO, V7X_LANES)
    outs = [
        ((n, bw), BF16),
        ((n, nqk), BF16),
        ((n, nqk), F32),
        ((n, bw), BF16),
        ((n, bw), F32),
        ((n, nqk), F32),
        ((n, bw), BF16),
        ((n, bw), BF16),
        ((n, bw), BF16),
        ((n, bw), BF16),
    ]
    consts = (g1, w, lng, lnb, sgw, sgbias, wa2, ba, cw, cb, clg, clb)
    return pl.pallas_call(
        _inproj_kernel,
        grid=(bsz, nt),
        in_specs=[pl.BlockSpec((t, d), row)] + [_const_spec(c.shape) for c in consts],
        out_specs=[pl.BlockSpec((t, s[1]), row) for s, _ in outs],
        out_shape=[jax.ShapeDtypeStruct(s, dt) for s, dt in outs],
        scratch_shapes=[pltpu.VMEM((t, bw), F32), pltpu.VMEM(slabs, F32), pltpu.VMEM(slabs, F32)],
        compiler_params=_cparams(56 << 20),
        name="in_proj",
    )(h, *consts)


def _gla_kernel(q_ref, k_ref, v_ref, r_ref, la_ref, ng_ref, y_ref,
                st_ref, stb_ref, upd_ref, later_ref, o_ref):
    tile = q_ref.shape[0]
    nchunk = tile // CHUNK
    pair_rows = 2 * GLA_DV
    npair = GLA_HEADS // 2

    sb = later_ref.shape[0]

    @pl.when(pl.program_id(1) == 0)
    def _():
        st_ref[...] = jnp.zeros_like(st_ref)
        ri = lax.broadcasted_iota(jnp.int32, (sb, sb), 0)
        ci = lax.broadcasted_iota(jnp.int32, (sb, sb), 1)
        later_ref[...] = jnp.where((ci > ri) & (ci // CHUNK == ri // CHUNK), 1.0, 0.0).astype(BF16)

    la = la_ref[...]
    hi = la.astype(BF16)
    lo = (la - hi.astype(F32)).astype(BF16)
    sfx = jnp.concatenate([_dot(later_ref[...], hi[r0:r0 + sb]) + _dot(later_ref[...], lo[r0:r0 + sb])
                           for r0 in range(0, tile, sb)], axis=0)
    kd = (k_ref[...] * jnp.exp(sfx)).astype(BF16)
    total = la + sfx

    own_lanes = (lax.broadcasted_iota(jnp.int32, (pair_rows, V7X_LANES), 1) // GLA_DK
                 == lax.broadcasted_iota(jnp.int32, (pair_rows, V7X_LANES), 0) // GLA_DV)

    for c in range(nchunk):
        rows = slice(c * CHUNK, (c + 1) * CHUNK)
        for p in range(npair):
            upd = _dot_tn(v_ref[rows, p * pair_rows:(p + 1) * pair_rows],
                          kd[rows, p * V7X_LANES:(p + 1) * V7X_LANES])
            upd_ref[c, p * pair_rows:(p + 1) * pair_rows, :] = jnp.where(own_lanes, upd, 0.0)

    st = [st_ref[p * pair_rows:(p + 1) * pair_rows, :] for p in range(npair)]
    for c in range(nchunk):
        decay = jnp.exp(total[c * CHUNK:c * CHUNK + 1, :])
        for p in range(npair):
            prow = slice(p * pair_rows, (p + 1) * pair_rows)
            st[p] = st[p] * decay[:, p * V7X_LANES:(p + 1) * V7X_LANES] + upd_ref[c, prow, :]
            stb_ref[c, prow, :] = st[p].astype(BF16)
    for p in range(npair):
        st_ref[p * pair_rows:(p + 1) * pair_rows, :] = st[p]

    for c in range(nchunk):
        rows = slice(c * CHUNK, (c + 1) * CHUNK)
        for p in range(npair):
            prow = slice(p * pair_rows, (p + 1) * pair_rows)
            o_ref[rows, prow] = _dot_nt(q_ref[rows, p * V7X_LANES:(p + 1) * V7X_LANES], stb_ref[c, prow, :])

    for h in range(GLA_HEADS):
        cols = slice(h * GLA_DV, (h + 1) * GLA_DV)
        o = o_ref[:, cols]
        o = o * lax.rsqrt(jnp.mean(o * o, axis=-1, keepdims=True) + EPS) * ng_ref[:, cols]
        y_ref[:, cols] = (o * _silu(r_ref[:, cols])).astype(y_ref.dtype)


def _gla(q, k, v, r, la, ng, bsz):
    n = q.shape[0]
    t = GLA_TILE
    nt = n // bsz // t
    nqk = GLA_HEADS * GLA_DK
    nv = GLA_HEADS * GLA_DV
    row = lambda b, i: (b * nt + i, 0)
    return pl.pallas_call(
        _gla_kernel,
        grid=(bsz, nt),
        in_specs=[pl.BlockSpec((t, nqk), row), pl.BlockSpec((t, nqk), row),
                  pl.BlockSpec((t, nv), row), pl.BlockSpec((t, nv), row),
                  pl.BlockSpec((t, nqk), row), _const_spec(ng.shape)],
        out_specs=pl.BlockSpec((t, nv), row),
        out_shape=jax.ShapeDtypeStruct((n, nv), BF16),
        scratch_shapes=[pltpu.VMEM((nv, V7X_LANES), F32),
                        pltpu.VMEM((t // CHUNK, nv, V7X_LANES), BF16),
                        pltpu.VMEM((t // CHUNK, nv, V7X_LANES), F32),
                        pltpu.VMEM((GLA_SUFFIX_ROWS, GLA_SUFFIX_ROWS), BF16), pltpu.VMEM((t, nv), F32)],
        compiler_params=_cparams(40 << 20),
        name="gla",
    )(q, k, v, r, la, ng)


def _attn_kernel(q_ref, kp_ref, kc_ref, vp_ref, vc_ref, bias_ref, *refs, n_cast):
    src_refs = refs[:n_cast]
    y_ref = refs[n_cast]
    dst_refs = refs[n_cast + 1:2 * n_cast + 1]
    s_ref, e_ref = refs[2 * n_cast + 1:]
    for src, dst in zip(src_refs, dst_refs):
        dst[...] = src[...].astype(dst.dtype)
    t = pl.program_id(1)
    tq = ATT_TILE

    def body(first_step):
        head_of_lane = lax.broadcasted_iota(jnp.int32, (1, ATT_GROUP * ATT_HD), 1) // ATT_HD
        for sub in range(ATT_SUB):
            pieces = []
            for back in (2, 1, 0):
                r0 = (sub - back) * tq
                pieces.append((kp_ref, vp_ref, r0 + ATT_SUB * tq, True) if r0 < 0
                              else (kc_ref, vc_ref, r0, False))
            qrows = slice(sub * tq, (sub + 1) * tq)
            for grp in range(ATT_HEADS // ATT_GROUP):
                cols = slice(grp * ATT_GROUP * ATT_HD, (grp + 1) * ATT_GROUP * ATT_HD)
                qg = q_ref[qrows, cols]
                for hh in range(ATT_GROUP):
                    mine = head_of_lane == hh
                    qm = jnp.where(mine, qg, jnp.zeros_like(qg))
                    head = grp * ATT_GROUP + hh
                    for j, (kr, _, r0, from_prev) in enumerate(pieces):
                        kc = slice(j * tq, (j + 1) * tq)
                        if first_step and from_prev:
                            s_ref[head, :, kc] = jnp.full((tq, tq), NEG_INF, F32)
                        else:
                            s_ref[head, :, kc] = _dot_nt(qm, kr[r0:r0 + tq, cols]) + bias_ref[head, :, kc]
                    e = jnp.exp2(s_ref[head] - jnp.max(s_ref[head], axis=-1, keepdims=True))
                    r = 1.0 / jnp.sum(e, axis=-1, keepdims=True)
                    e_ref[head] = e.astype(BF16)
                    pv = sum(_dot(e_ref[head, :, j * tq:(j + 1) * tq], vr[r0:r0 + tq, cols])
                             for j, (_, vr, r0, _) in enumerate(pieces)) * r
                    hd = slice(hh * ATT_HD, (hh + 1) * ATT_HD)
                    y_ref[qrows, cols.start + hd.start:cols.start + hd.stop] = pv[:, hd].astype(y_ref.dtype)

    pl.when(t == 0)(functools.partial(body, True))
    pl.when(t > 0)(functools.partial(body, False))


def _attn(q, k, v, bias, bsz, to_cast, layer):
    n, w = q.shape
    t = ATT_SUB * ATT_TILE
    nt = n // bsz // t
    steps = bsz * nt
    cur = lambda b, i: (b * nt + i, 0)
    prev = lambda b, i: (b * nt + jnp.maximum(i - 1, 0), 0)
    blk = lambda im: pl.BlockSpec((t, w), im)
    for m in to_cast:
        assert m.shape[1] % (steps * BF16_SUBLANES) == 0, m.shape
    src_slab = lambda m: pl.BlockSpec((None, m.shape[1] // steps, m.shape[2]),
                                      lambda b, i: (layer, b * nt + i, 0))
    dst_slab = lambda m: pl.BlockSpec((m.shape[1] // steps, m.shape[2]), cur)
    outs = pl.pallas_call(
        functools.partial(_attn_kernel, n_cast=len(to_cast)),
        grid=(bsz, nt),
        in_specs=[blk(cur), blk(prev), blk(cur), blk(prev), blk(cur), _const_spec(bias.shape)]
                 + [src_slab(m) for m in to_cast],
        out_specs=[blk(cur)] + [dst_slab(m) for m in to_cast],
        out_shape=[jax.ShapeDtypeStruct((n, w), BF16)]
                  + [jax.ShapeDtypeStruct(m.shape[1:], BF16) for m in to_cast],
        scratch_shapes=[pltpu.VMEM((ATT_HEADS, ATT_TILE, ATT_WIN), F32),
                        pltpu.VMEM((ATT_HEADS, ATT_TILE, ATT_WIN), BF16)],
        compiler_params=_cparams(56 << 20),
        name="attn",
    )(q, k, k, v, v, bias, *to_cast)
    return outs[0], outs[1:]


def _att_bias_table(rel_bias):
    nh, ntab = rel_bias.shape
    period = ATT_TILE + ATT_WIN
    n_lo = (ATT_TILE - 1) - (CHUNK - 1)
    rb = rel_bias.astype(F32)
    vec = jnp.concatenate([jnp.broadcast_to(rb[:, :1], (nh, n_lo)), rb,
                           jnp.broadcast_to(rb[:, -1:], (nh, period - n_lo - ntab))], axis=1)
    rolled = jnp.roll(vec[:, ::-1], -ATT_TILE, axis=1)
    flat = jnp.tile(rolled, (1, ATT_TILE))[:, :ATT_TILE * (period - 1)]
    table = flat.reshape(nh, ATT_TILE, period - 1)[:, :, :ATT_WIN]
    qc = jnp.arange(ATT_TILE)[:, None] // CHUNK + 2 * ATT_TILE // CHUNK
    kc = jnp.arange(ATT_WIN)[None, :] // CHUNK
    in_band = (kc <= qc) & (kc >= qc - (ATT_BAND - 1))
    return jnp.where(in_band[None], table * LOG2E, NEG_INF)


def _conv_tile(x_ref, w_ref, b_ref, lng_ref, lnb_ref, ext_ref, co_ref, y_ref):
    tile, width = x_ref.shape
    nslab = width // V7X_LANES
    out0 = CONV_HALO + tile - 8 * CONV_SEG
    in0 = out0 - (CONV_K - 1)
    lanes = [slice(c * V7X_LANES, (c + 1) * V7X_LANES) for c in range(nslab)]

    @pl.when(pl.program_id(1) == 0)
    def _():
        ext_ref[:, 0:CONV_HALO, :] = jnp.zeros((nslab, CONV_HALO, V7X_LANES), F32)

    for c in range(nslab):
        ext_ref[c, CONV_HALO:, :] = x_ref[:, lanes[c]]

    for c in range(nslab):
        for i0 in range(0, CONV_SEG, CONV_IB):
            accs = [None] * CONV_IB
            for j in range(CONV_K):
                wj = jnp.broadcast_to(w_ref[j:j + 1, lanes[c]], (8, V7X_LANES))
                for a in range(CONV_IB):
                    term = wj * ext_ref[c, pl.ds(in0 + i0 + a + j, 8, stride=CONV_SEG), :]
                    accs[a] = term if accs[a] is None else accs[a] + term
            for a in range(CONV_IB):
                co_ref[c, pl.ds(out0 + i0 + a, 8, stride=CONV_SEG), :] = accs[a]

    for r0 in range(0, tile, CONV_NB):
        parts = [co_ref[c, CONV_HALO + r0:CONV_HALO + r0 + CONV_NB, :] + b_ref[:, lanes[c]]
                 for c in range(nslab)]
        mu = jnp.sum(sum(parts), axis=-1, keepdims=True) / width
        cen = [q - mu for q in parts]
        var = jnp.sum(sum(q * q for q in cen), axis=-1, keepdims=True) / width
        inv = lax.rsqrt(var + EPS)
        for c in range(nslab):
            y = cen[c] * inv * lng_ref[:, lanes[c]] + lnb_ref[:, lanes[c]]
            y_ref[r0:r0 + CONV_NB, lanes[c]] = _silu(y).astype(y_ref.dtype)

    for c in range(nslab):
        ext_ref[c, 0:CONV_HALO, :] = ext_ref[c, tile:tile + CONV_HALO, :]


def _merge_kernel(h_ref, ya_ref, yb_ref, yc_ref, yd_ref, g1_ref, wg_ref, bg_ref, wb_ref, wo_ref, o_ref,
                  xn_ref):
    xn_ref[...] = _rms(h_ref[...], g1_ref[...]).astype(BF16)
    merged = None
    for n, y_ref in enumerate((ya_ref, yb_ref, yc_ref, yd_ref)):
        gate = _sigmoid(_dot(xn_ref[...], wg_ref[n]) + bg_ref[n])
        term = gate * _dot(y_ref[...], wb_ref[n])
        merged = term if merged is None else merged + term
    o_ref[...] = h_ref[...] + _dot(merged.astype(BF16), wo_ref[...])


def _merge(h, ys, g1, wg, bg, wb, wo):
    n, d = h.shape
    t = DENSE_TILE
    row = lambda i: (i, 0)
    consts = (g1, wg, bg, wb, wo)
    return pl.pallas_call(
        _merge_kernel,
        grid=(n // t,),
        in_specs=[pl.BlockSpec((t, d), row)] + [pl.BlockSpec((t, y.shape[1]), row) for y in ys]
                 + [_const_spec(c.shape) for c in consts],
        out_specs=pl.BlockSpec((t, d), row),
        out_shape=jax.ShapeDtypeStruct((n, d), F32),
        scratch_shapes=[pltpu.VMEM((t, d), BF16)],
        compiler_params=_cparams(58 << 20),
        name="merge",
    )(h, *ys, *consts)


def _ffn_ple_kernel(h_ref, p_ref, g2_ref, w1_ref, w2_ref, g3_ref, wpg_ref, bpg_ref, wp_ref, gf_ref,
                    o_ref, hn_ref, *, final_norm, ff_chunk):
    hn_ref[...] = _rms(h_ref[...], g2_ref[...]).astype(BF16)
    acc = None
    for j in range(w1_ref.shape[1] // ff_chunk):
        cols = slice(j * ff_chunk, (j + 1) * ff_chunk)
        a = jnp.maximum(_dot(hn_ref[...], w1_ref[:, cols]), 0.0)
        acc = (h_ref[...] if acc is None else acc) + _dot((a * a).astype(BF16), w2_ref[cols, :])
    hg = _rms(acc, g3_ref[...]).astype(BF16)
    gate = _sigmoid(_dot(hg, wpg_ref[...]) + bpg_ref[...])
    out = acc + gate * _dot(p_ref[...].astype(BF16), wp_ref[...])
    if final_norm:
        out = _rms(out, gf_ref[...])
    o_ref[...] = out


def _ffn_ple(h, p, layer, g2, w1, w2, g3, wpg, bpg, wp, gf, final_norm):
    n, d = h.shape
    t = DENSE_TILE
    row = lambda i: (i, 0)
    consts = (g2, w1, w2, g3, wpg, bpg, wp, gf)
    return pl.pallas_call(
        functools.partial(_ffn_ple_kernel, final_norm=final_norm, ff_chunk=1024),
        grid=(n // t,),
        in_specs=[pl.BlockSpec((t, d), row),
                  pl.BlockSpec((None, t, p.shape[2]), lambda i: (layer, i, 0))]
                 + [_const_spec(c.shape) for c in consts],
        out_specs=pl.BlockSpec((t, d), row),
        out_shape=jax.ShapeDtypeStruct((n, d), F32),
        scratch_shapes=[pltpu.VMEM((t, d), BF16)],
        compiler_params=_cparams(58 << 20),
        name="ffn_ple",
    )(h, p, *consts)


def kernel(x, p, norm1_g, w_in, sg_ln_g, sg_ln_b, sg_w, sg_b, gla_w_a2, gla_b_a, gla_norm_g, att_rel_bias, conv_dw_w, conv_dw_b, conv_ln_g, conv_ln_b, w_branch, w_gate, b_gate, w_out, norm2_g, w_ff1, w_ff2, norm3_g, w_ple_gate, b_ple_gate, w_ple, final_g):
    bsz, seq, d = x.shape
    depth = w_in.shape[0]
    n = bsz * seq
    bw = d // 2
    nqk = GLA_HEADS * GLA_DK
    assert seq % TOK_TILE == 0 and seq % (ATT_SUB * ATT_TILE) == 0 and n % DENSE_TILE == 0
    assert seq % GLA_TILE == 0 and GLA_TILE % GLA_SUFFIX_ROWS == 0 and GLA_SUFFIX_ROWS % CHUNK == 0
    assert bw == SG_GROUPS * V7X_LANES
    row2 = lambda a: a.reshape(1, -1)

    c_ga = 2 * bw + 2 * nqk + 2 * bw
    c_att = c_ga + GLA_RANK

    h = x.reshape(n, d)
    p3 = p.reshape(depth, n, -1)
    for i in range(depth):
        w = jnp.concatenate([w_in[i, :, :c_ga], w_in[i, :, c_att:],
                             jnp.pad(w_in[i, :, c_ga:c_att], ((0, 0), (0, V7X_LANES - GLA_RANK)))],
                            axis=1).astype(BF16)
        wa2 = jnp.pad(gla_w_a2[i].astype(BF16), ((0, V7X_LANES - GLA_RANK), (0, 0)))
        sgbias = jnp.repeat(sg_b[i].T, bw // SG_GROUPS, axis=1)
        g1 = row2(norm1_g[i])
        (y_a, gq, gk, gv, gr, la, aq, ak, av, y_d) = _in_proj(
            h, g1, w, row2(sg_ln_g[i]), row2(sg_ln_b[i]), sg_w[i], sgbias, wa2, row2(gla_b_a[i]),
            jnp.pad(conv_dw_w[i], ((0, 1), (0, 0))), row2(conv_dw_b[i]),
            row2(conv_ln_g[i]), row2(conv_ln_b[i]), bsz)
        y_b = _gla(gq, gk, gv, gr, la, row2(gla_norm_g[i]), bsz)
        y_c, (wg, wb, wo, w1, w2, wpg) = _attn(
            aq, ak, av, _att_bias_table(att_rel_bias[i]), bsz,
            (w_gate.reshape(depth, -1, d), w_branch.reshape(depth, -1, d), w_out, w_ff1, w_ff2,
             w_ple_gate), i)
        h = _merge(h, (y_a, y_b, y_c, y_d), g1, wg.reshape(w_gate.shape[1:]), b_gate[i][:, None, :],
                   wb.reshape(w_branch.shape[1:]), wo)
        h = _ffn_ple(h, p3, i, row2(norm2_g[i]), w1, w2, row2(norm3_g[i]), wpg,
                     row2(b_ple_gate[i]), w_ple[i].astype(BF16), row2(final_g),
                     final_norm=(i == depth - 1))
    return h.reshape(bsz, seq, d)
```
